```python
import jax, jax.numpy as jnp
from jax import lax
import numpy as np

D_MODEL = 1024
BATCH = 4
SEQ = 8192
DEPTH = 1
DEC_BATCH = 4
DEC_SEQ = 4096
PAST_LEN = 128

N_META = 16
GRID_W = 64
NA_HEADS = 8
NA_HEAD_DIM = 64
NA_WIDTH = NA_HEADS * NA_HEAD_DIM
NA_WIN_ROWS = 8
NA_WIN_COLS = 16
FN_GROUPS = 4
FN_GROUP_DIM = 128
FN_WIDTH = FN_GROUPS * FN_GROUP_DIM
D_FF = 4 * D_MODEL
RMS_EPS = 1e-6
IN_WIDTH = 3 * NA_WIDTH + FN_WIDTH + 2 * D_MODEL

kernel_name = 'hybrid_natten_fnet_encoder'


def rmsnorm(x, g):
    xf = x.astype(jnp.float32)
    y = xf * lax.rsqrt(jnp.mean(xf * xf, axis=-1, keepdims=True) + RMS_EPS)
    return (y * g.astype(jnp.float32)).astype(x.dtype)


def neighbourhood_attention(q, k, v, rel_bias, meta_bias):
    B, N, H, Dh = q.shape
    T = N - N_META
    rows = T // GRID_W
    wr = min(NA_WIN_ROWS, rows)
    scale = Dh ** -0.5
    qm, km, vm = q[:, :N_META], k[:, :N_META], v[:, :N_META]
    qg = q[:, N_META:].reshape(B, rows, GRID_W, H, Dh)
    kg = k[:, N_META:].reshape(B, rows, GRID_W, H, Dh)
    vg = v[:, N_META:].reshape(B, rows, GRID_W, H, Dh)

    s_mm = jnp.einsum('bqhd,bkhd->bhqk', qm, km) * scale + meta_bias[None, :, None, :]
    p_mm = jax.nn.softmax(s_mm.astype(jnp.float32), axis=-1).astype(v.dtype)
    o_meta = jnp.einsum('bhqk,bkhd->bqhd', p_mm, vm)

    cols = jnp.arange(GRID_W)
    col_start = jnp.clip(cols - NA_WIN_COLS // 2, 0, GRID_W - NA_WIN_COLS)
    col_idx = col_start[:, None] + jnp.arange(NA_WIN_COLS)[None, :]
    col_off = col_idx - cols[:, None]
    col_bias = rel_bias[:, :, col_off + NA_WIN_COLS - 1]

    def row_block(r):
        rs = jnp.clip(r - wr // 2, 0, rows - wr)
        q_r = lax.dynamic_index_in_dim(qg, r, axis=1, keepdims=False)
        k_rows = lax.dynamic_slice_in_dim(kg, rs, wr, axis=1)
        v_rows = lax.dynamic_slice_in_dim(vg, rs, wr, axis=1)
        k_win = k_rows[:, :, col_idx]
        v_win = v_rows[:, :, col_idx]
        row_off = rs + jnp.arange(wr) - r
        bias = col_bias[:, row_off + NA_WIN_ROWS - 1]
        s_loc = jnp.einsum('bchd,bwcjhd->bhcwj', q_r, k_win) * scale + jnp.transpose(bias, (0, 2, 1, 3))[None]
        s_loc = s_loc.reshape(B, H, GRID_W, wr * NA_WIN_COLS)
        s_met = jnp.einsum('bchd,bmhd->bhcm', q_r, km) * scale + meta_bias[None, :, None, :]
        s = jnp.concatenate([s_loc, s_met], axis=-1).astype(jnp.float32)
        p = jax.nn.softmax(s, axis=-1).astype(v.dtype)
        p_loc = p[..., :wr * NA_WIN_COLS].reshape(B, H, GRID_W, wr, NA_WIN_COLS)
        p_met = p[..., wr * NA_WIN_COLS:]
        return (jnp.einsum('bhcwj,bwcjhd->bchd', p_loc, v_win)
                + jnp.einsum('bhcm,bmhd->bchd', p_met, vm))

    o_grid = lax.map(row_block, jnp.arange(rows))
    o_grid = jnp.transpose(o_grid, (1, 0, 2, 3, 4)).reshape(B, T, H, Dh)
    return jnp.concatenate([o_meta, o_grid], axis=1)


def fourier_mix(u):
    B, N, _ = u.shape
    ug = u.reshape(B, N, FN_GROUPS, FN_GROUP_DIM).astype(jnp.float32)
    f = jnp.fft.fft2(ug, axes=(1, 3), norm='ortho').real
    return f.reshape(B, N, FN_WIDTH).astype(u.dtype)


def encoder_layer(x, w_in, rel_bias, meta_bias, w_branch_na, w_branch_fn, w_out, g_mix, g_mlp, w_up, w_down):
    B, N, _ = x.shape
    h = rmsnorm(x, g_mix)
    z = h @ w_in
    q, k, v, u, gate_na, gate_fn = jnp.split(
        z, [NA_WIDTH, 2 * NA_WIDTH, 3 * NA_WIDTH, 3 * NA_WIDTH + FN_WIDTH,
            3 * NA_WIDTH + FN_WIDTH + D_MODEL], axis=-1)
    q = q.reshape(B, N, NA_HEADS, NA_HEAD_DIM)
    k = k.reshape(B, N, NA_HEADS, NA_HEAD_DIM)
    v = v.reshape(B, N, NA_HEADS, NA_HEAD_DIM)
    y_na = neighbourhood_attention(q, k, v, rel_bias, meta_bias).reshape(B, N, NA_WIDTH) @ w_branch_na
    y_fn = fourier_mix(u) @ w_branch_fn
    mixed = jax.nn.sigmoid(gate_na) * y_na + jax.nn.sigmoid(gate_fn) * y_fn
    x = x + mixed @ w_out
    a = jax.nn.relu(rmsnorm(x, g_mlp) @ w_up)
    return x + (a * a) @ w_down


def run_trunk(x, meta_tokens, w_in, rel_bias, meta_bias, w_branch_na, w_branch_fn, w_out,
              g_mix, g_mlp, w_up, w_down, g_final):
    B = x.shape[0]
    meta = jnp.broadcast_to(meta_tokens.astype(x.dtype)[None], (B, N_META, D_MODEL))
    h = jnp.concatenate([meta, x], axis=1)
    for l in range(DEPTH):
        h = encoder_layer(h, w_in[l], rel_bias[l], meta_bias[l], w_branch_na[l], w_branch_fn[l],
                          w_out[l], g_mix[l], g_mlp[l], w_up[l], w_down[l])
    return rmsnorm(h, g_final)[:, N_META:]


def setup_inputs(seed: int = 0) -> dict:
    key = jax.random.key(seed)
    ks = jax.random.split(key, 16)
    L, D = DEPTH, D_MODEL

    def nrm(k, shape, scale):
        return jax.random.normal(k, shape, jnp.float32) * scale

    return {
        'x_prompt': nrm(ks[0], (BATCH, SEQ, D), 1.0),
        'x_sample': nrm(ks[1], (DEC_BATCH, DEC_SEQ, D), 1.0),
        'meta_tokens': nrm(ks[2], (N_META, D), 1.0),
        'w_in': nrm(ks[3], (L, D, IN_WIDTH), D ** -0.5),
        'rel_bias': nrm(ks[4], (L, NA_HEADS, 2 * NA_WIN_ROWS - 1, 2 * NA_WIN_COLS - 1), 0.1),
        'meta_bias': nrm(ks[5], (L, NA_HEADS, N_META), 0.1),
        'w_branch_na': nrm(ks[6], (L, NA_WIDTH, D), NA_WIDTH ** -0.5),
        'w_branch_fn': nrm(ks[7], (L, FN_WIDTH, D), FN_WIDTH ** -0.5),
        'w_out': nrm(ks[8], (L, D, D), D ** -0.5),
        'g_mix': 1.0 + nrm(ks[9], (L, D), 0.02),
        'g_mlp': 1.0 + nrm(ks[10], (L, D), 0.02),
        'w_up': nrm(ks[11], (L, D, D_FF), D ** -0.5),
        'w_down': nrm(ks[12], (L, D_FF, D), D_FF ** -0.5),
        'g_final': 1.0 + nrm(ks[13], (D,), 0.02),
    }


def reference(x_prompt, x_sample, meta_tokens, w_in, rel_bias, meta_bias, w_branch_na, w_branch_fn,
              w_out, g_mix, g_mlp, w_up, w_down, g_final):
    y_prompt = run_trunk(x_prompt, meta_tokens, w_in, rel_bias, meta_bias, w_branch_na, w_branch_fn,
                         w_out, g_mix, g_mlp, w_up, w_down, g_final)
    y_sample = run_trunk(x_sample, meta_tokens, w_in, rel_bias, meta_bias, w_branch_na, w_branch_fn,
                         w_out, g_mix, g_mlp, w_up, w_down, g_final)
    return (y_prompt, y_sample)
```

```python
import functools
import math

import jax
import jax.numpy as jnp
import numpy as np
from jax import lax
from jax.experimental import pallas as pl
from jax.experimental.pallas import tpu as pltpu

F32 = jnp.float32
BF16 = jnp.bfloat16

D_MODEL = 1024
N_META = 16
GRID_W = 64
NA_HEADS = 8
NA_HEAD_DIM = 64
NA_WIDTH = NA_HEADS * NA_HEAD_DIM
NA_WIN_ROWS = 8
NA_WIN_COLS = 16
FN_GROUPS = 4
FN_GROUP_DIM = 128
FN_WIDTH = FN_GROUPS * FN_GROUP_DIM
D_FF = 4 * D_MODEL
RMS_EPS = 1e-6

LANES = 128
HEAD_PAIRS = NA_HEADS // 2
WIN_SLOTS = NA_WIN_ROWS + 2
WIN_KEYS = WIN_SLOTS * GRID_W
LOCAL_KEYS = (NA_WIN_ROWS + 1) * GRID_W
NEG = -1e30
DFT_OUTER = 16
VMEM_LIMIT = 56 * 1024 * 1024


def _rms(x, g):
    ms = jnp.mean(x * x, axis=-1, keepdims=True)
    return x * lax.rsqrt(ms + RMS_EPS) * g


def _inproj_body(x_ref, g_ref, w_ref, cs_ref, q_ref, k_ref, v_ref, a_ref, b_ref):
    h = _rms(x_ref[0], g_ref[...]).astype(BF16)
    z = jnp.dot(h, w_ref[...], preferred_element_type=F32)
    q_ref[0] = (z[:, :NA_WIDTH] * (NA_HEAD_DIM ** -0.5)).astype(BF16)
    k_ref[0] = z[:, NA_WIDTH:2 * NA_WIDTH].astype(BF16)
    v_ref[0] = z[:, 2 * NA_WIDTH:3 * NA_WIDTH].astype(BF16)
    u = z[:, 3 * NA_WIDTH:].astype(BF16)
    ab = jnp.dot(u, cs_ref[...], preferred_element_type=F32)
    a_ref[0] = ab[:, :FN_WIDTH].astype(BF16)
    b_ref[0] = ab[:, FN_WIDTH:].astype(BF16)


def _inproj(x, g, w, cs, tm):
    B, T, D = x.shape
    out = jax.ShapeDtypeStruct((B, T, NA_WIDTH), BF16)
    tok = pl.BlockSpec((1, tm, NA_WIDTH), lambda b, i: (b, i, 0))
    return pl.pallas_call(
        _inproj_body,
        grid=(B, T // tm),
        in_specs=[
            pl.BlockSpec((1, tm, D), lambda b, i: (b, i, 0)),
            pl.BlockSpec((1, D), lambda b, i: (0, 0)),
            pl.BlockSpec(w.shape, lambda b, i: (0, 0)),
            pl.BlockSpec(cs.shape, lambda b, i: (0, 0)),
        ],
        out_specs=[tok] * 5,
        out_shape=[out] * 5,
        compiler_params=pltpu.CompilerParams(
            dimension_semantics=("arbitrary", "arbitrary"), vmem_limit_bytes=VMEM_LIMIT),
        name="inproj",
    )(x, g, w, cs)


def _attn_body(q_ref, k_ref, v_ref, km_ref, vm_ref, bias_ref, o_ref, *, rows, rows_per_step):
    chunk = pl.program_id(2)
    lane = lax.broadcasted_iota(jnp.int32, (2 * GRID_W, LANES), 1)
    first_head = lane < NA_HEAD_DIM
    km = km_ref[...]
    vm = vm_ref[...]
    zero = jnp.zeros((2 * GRID_W, LANES), BF16)

    def pair(i, carry):
        r = chunk * rows_per_step + 2 * i
        start = jnp.clip(r - NA_WIN_ROWS // 2, 0, rows - (NA_WIN_ROWS + 1))
        cfg = jnp.where(r == 0, 0, jnp.where(r == 2, 1, jnp.where(
            r == rows - 4, 3, jnp.where(r == rows - 2, 4, 2))))
        q2 = q_ref[0, pl.ds(pl.multiple_of(i * 2 * GRID_W, 2 * GRID_W), 2 * GRID_W), :]
        qq = jnp.concatenate([jnp.where(first_head, q2, zero), jnp.where(first_head, zero, q2)], axis=0)
        koff = pl.multiple_of(start * GRID_W, GRID_W)
        kw = jnp.concatenate([k_ref[0, pl.ds(koff, LOCAL_KEYS), :], km], axis=0)
        vw = jnp.concatenate([v_ref[0, pl.ds(koff, LOCAL_KEYS), :], vm], axis=0)
        s = lax.dot_general(qq, kw, (((1,), (1,)), ((), ())), preferred_element_type=F32)
        s = s + bias_ref[0, cfg]
        m = jnp.max(s, axis=-1, keepdims=True)
        p = jnp.exp(s - m)
        l = jnp.sum(p, axis=-1, keepdims=True)
        o = jnp.dot(p.astype(BF16), vw, preferred_element_type=F32) / l
        o2 = jnp.where(first_head, o[:2 * GRID_W], o[2 * GRID_W:])
        o_ref[0, pl.ds(pl.multiple_of(i * 2 * GRID_W, 2 * GRID_W), 2 * GRID_W), :] = o2.astype(BF16)
        return carry

    lax.fori_loop(0, rows_per_step // 2, pair, 0)


def _attention(q, k, v, km, vm, bias, rows_per_step):
    B, T, _ = q.shape
    rows = T // GRID_W
    tq = rows_per_step * GRID_W
    body = functools.partial(_attn_body, rows=rows, rows_per_step=rows_per_step)
    return pl.pallas_call(
        body,
        grid=(B, HEAD_PAIRS, rows // rows_per_step),
        in_specs=[
            pl.BlockSpec((1, tq, LANES), lambda b, h, c: (b, c, h)),
            pl.BlockSpec((1, T, LANES), lambda b, h, c: (b, 0, h)),
            pl.BlockSpec((1, T, LANES), lambda b, h, c: (b, 0, h)),
            pl.BlockSpec((GRID_W, LANES), lambda b, h, c: (0, h)),
            pl.BlockSpec((GRID_W, LANES), lambda b, h, c: (0, h)),
            pl.BlockSpec((1,) + bias.shape[1:], lambda b, h, c: (h, 0, 0, 0)),
        ],
        out_specs=pl.BlockSpec((1, tq, LANES), lambda b, h, c: (b, c, h)),
        out_shape=jax.ShapeDtypeStruct((B, T, NA_WIDTH), BF16),
        compiler_params=pltpu.CompilerParams(
            dimension_semantics=("arbitrary", "arbitrary", "arbitrary"), vmem_limit_bytes=VMEM_LIMIT),
        name="attention",
    )(q, k, v, km, vm, bias)


def _bias_tables(rel_bias, meta_bias):
    c = np.arange(GRID_W)[:, None]
    j = np.arange(GRID_W)[None, :]
    cs = np.clip(c - NA_WIN_COLS // 2, 0, GRID_W - NA_WIN_COLS)
    valid = (j >= cs) & (j < cs + NA_WIN_COLS)
    co = np.clip(j - c + NA_WIN_COLS - 1, 0, 2 * NA_WIN_COLS - 2)
    toe = jnp.where(valid[None, None], rel_bias[:, :, co], NEG)
    negblk = jnp.full((NA_HEADS, GRID_W, GRID_W), NEG, F32)
    metablk = jnp.concatenate(
        [jnp.broadcast_to(meta_bias[:, None, :], (NA_HEADS, GRID_W, N_META)),
         jnp.full((NA_HEADS, GRID_W, GRID_W - N_META), NEG, F32)], axis=-1)
    cfgs = [((0, 1), ((0, 8), (0, 8))), ((2, 3), ((0, 8), (0, 8))), ((4, 5), ((0, 8), (1, 9))),
            ((5, 6), ((1, 9), (1, 9))), ((7, 8), ((1, 9), (1, 9)))]
    per_cfg = []
    for qslots, wins in cfgs:
        per_row = []
        for a, (w0, w1) in zip(qslots, wins):
            blocks = [toe[:, w - a + NA_WIN_ROWS - 1] if w0 <= w < w1 else negblk
                      for w in range(NA_WIN_ROWS + 1)]
            per_row.append(jnp.concatenate(blocks + [metablk], axis=-1))
        per_cfg.append(jnp.concatenate(per_row, axis=1))
    t = jnp.stack(per_cfg, axis=1)
    t = t.reshape(HEAD_PAIRS, 2, len(cfgs), 2 * GRID_W, WIN_KEYS)
    return jnp.transpose(t, (0, 2, 1, 3, 4)).reshape(HEAD_PAIRS, len(cfgs), 4 * GRID_W, WIN_KEYS)


def _dft16_body(a_ref, b_ref, wa_ref, wb_ref, z_ref):
    z = jnp.dot(wa_ref[...], a_ref[0], preferred_element_type=F32)
    z = z + jnp.dot(wb_ref[...], b_ref[0], preferred_element_type=F32)
    z_ref[0] = z.astype(BF16)


def _dft16(a, b, wa, wb, cw):
    B, _, L = a.shape
    col = pl.BlockSpec((1, DFT_OUTER, cw), lambda i, j: (i, 0, j))
    wsp = pl.BlockSpec(wa.shape, lambda i, j: (0, 0))
    return pl.pallas_call(
        _dft16_body,
        grid=(B, L // cw),
        in_specs=[col, col, wsp, wsp],
        out_specs=pl.BlockSpec((1, 2 * DFT_OUTER, cw), lambda i, j: (i, 0, j)),
        out_shape=jax.ShapeDtypeStruct((B, 2 * DFT_OUTER, L), BF16),
        compiler_params=pltpu.CompilerParams(
            dimension_semantics=("arbitrary", "arbitrary"), vmem_limit_bytes=VMEM_LIMIT),
        name="dft16",
    )(a, b, wa, wb)


def _dftn2_body(zr_ref, zi_ref, g_ref, y_ref, *, kmain):
    zr = zr_ref[0, 0]
    zi = zi_ref[0, 0]
    gr = g_ref[0, 0]
    gi = g_ref[0, 1]
    y = jnp.dot(gr[:, :kmain], zr[:kmain], preferred_element_type=F32)
    y = y + jnp.dot(gi[:, :kmain], zi[:kmain], preferred_element_type=F32)
    y = y + gr[:, kmain:].astype(F32) * zr[kmain:].astype(F32)
    y = y + gi[:, kmain:].astype(F32) * zi[kmain:].astype(F32)
    y_ref[0] = y.astype(BF16)


def _dftn2(z4, g):
    B, _, N2, C = z4.shape
    body = functools.partial(_dftn2_body, kmain=N2 - 1)
    return pl.pallas_call(
        body,
        grid=(DFT_OUTER, B),
        in_specs=[
            pl.BlockSpec((1, 1, N2, C), lambda c, b: (b, c, 0, 0)),
            pl.BlockSpec((1, 1, N2, C), lambda c, b: (b, c + DFT_OUTER, 0, 0)),
            pl.BlockSpec((1, 2, N2 - 1, N2), lambda c, b: (c, 0, 0, 0)),
        ],
        out_specs=pl.BlockSpec((1, N2 - 1, C), lambda c, b: (b, 0, c)),
        out_shape=jax.ShapeDtypeStruct((B, N2 - 1, DFT_OUTER * C), BF16),
        compiler_params=pltpu.CompilerParams(
            dimension_semantics=("arbitrary", "arbitrary"), vmem_limit_bytes=VMEM_LIMIT),
        name="dftn2",
    )(z4, z4, g)


def _dft_constants(n2):
    n = DFT_OUTER * n2
    a = jnp.arange(DFT_OUTER, dtype=jnp.int32)
    phi = ((a[:, None] * a[None, :]) % DFT_OUTER).astype(F32) * (2.0 * math.pi / DFT_OUTER)
    cphi, sphi = jnp.cos(phi), jnp.sin(phi)
    wa = jnp.concatenate([cphi, -sphi], axis=0).astype(BF16)
    wb = jnp.concatenate([-sphi, -cphi], axis=0).astype(BF16)
    c = jnp.arange(DFT_OUTER, dtype=jnp.int32)[:, None, None]
    d = jnp.arange(1, n2, dtype=jnp.int32)[None, :, None]
    b = jnp.arange(n2, dtype=jnp.int32)[None, None, :]
    theta = ((b * (c + DFT_OUTER * d)) % n).astype(F32) * (2.0 * math.pi / n)
    scale = 1.0 / math.sqrt(n)
    g = jnp.stack([jnp.cos(theta) * scale, jnp.sin(theta) * scale], axis=1).astype(BF16)
    return wa, wb, g


def _channel_dft_matrix():
    ch = jnp.arange(FN_GROUP_DIM, dtype=jnp.int32)
    th = ((ch[:, None] * ch[None, :]) % FN_GROUP_DIM).astype(F32) * (2.0 * math.pi / FN_GROUP_DIM)
    scale = 1.0 / math.sqrt(FN_GROUP_DIM)
    eye = jnp.eye(FN_GROUPS, dtype=F32)
    cblk = jnp.kron(eye, jnp.cos(th) * scale)
    sblk = jnp.kron(eye, jnp.sin(th) * scale)
    return jnp.concatenate([cblk, sblk], axis=1).astype(BF16)


def _column_chunk(length, limit=33000):
    units = length // LANES
    best = 1
    for kk in range(1, units + 1):
        if units % kk == 0 and kk * LANES <= limit:
            best = kk
    return best * LANES


def _trunk_body(x_ref, o_ref, f_ref, gmix_ref, wg_ref, wna_ref, wfn_ref, wout_ref, gmlp_ref,
                wup_ref, wdn_ref, gfin_ref, y_ref, *, ff_chunk):
    x = x_ref[0]
    h = _rms(x, gmix_ref[...]).astype(BF16)
    gates = jnp.dot(h, wg_ref[...], preferred_element_type=F32)
    y_na = jnp.dot(o_ref[0], wna_ref[...], preferred_element_type=F32)
    y_fn = jnp.dot(f_ref[0], wfn_ref[...], preferred_element_type=F32)
    mixed = jax.nn.sigmoid(gates[:, :D_MODEL]) * y_na + jax.nn.sigmoid(gates[:, D_MODEL:]) * y_fn
    x1 = x + jnp.dot(mixed.astype(BF16), wout_ref[...], preferred_element_type=F32)
    h2 = _rms(x1, gmlp_ref[...]).astype(BF16)
    acc = x1
    for c in range(D_FF // ff_chunk):
        up = jnp.dot(h2, wup_ref[:, c * ff_chunk:(c + 1) * ff_chunk], preferred_element_type=F32)
        up = jnp.maximum(up, 0.0)
        acc = acc + jnp.dot((up * up).astype(BF16), wdn_ref[c * ff_chunk:(c + 1) * ff_chunk, :],
                            preferred_element_type=F32)
    y_ref[0] = _rms(acc, gfin_ref[...])


def _trunk(x, o, f, gmix, wg, wna, wfn, wout, gmlp, wup, wdn, gfin, tm):
    B, T, D = x.shape
    body = functools.partial(_trunk_body, ff_chunk=1024)

    def const(arr):
        return pl.BlockSpec(arr.shape, lambda b, i: (0, 0), pipeline_mode=pl.Buffered(1))

    return pl.pallas_call(
        body,
        grid=(B, T // tm),
        in_specs=[
            pl.BlockSpec((1, tm, D), lambda b, i: (b, i, 0)),
            pl.BlockSpec((1, tm, NA_WIDTH), lambda b, i: (b, i, 0)),
            pl.BlockSpec((1, tm, FN_WIDTH), lambda b, i: (b, i, 0)),
            const(gmix), const(wg), const(wna), const(wfn), const(wout), const(gmlp),
            const(wup), const(wdn), const(gfin),
        ],
        out_specs=pl.BlockSpec((1, tm, D), lambda b, i: (b, i, 0)),
        out_shape=jax.ShapeDtypeStruct((B, T, D), F32),
        compiler_params=pltpu.CompilerParams(
            dimension_semantics=("arbitrary", "arbitrary"), vmem_limit_bytes=VMEM_LIMIT),
        name="trunk",
    )(x, o, f, gmix, wg, wna, wfn, wout, gmlp, wup, wdn, gfin)


def _pad_rows(a, rows):
    return jnp.concatenate([a, jnp.zeros((rows - a.shape[0], a.shape[1]), a.dtype)], axis=0)


def _run_group(x, meta, consts):
    (w_qkvu, cs, bias, g_mix, wg, wna, wfn, wout, g_mlp, wup, wdn, g_fin) = consts
    B, T, _ = x.shape
    n2 = (N_META + T) // DFT_OUTER
    q, k, v, a, b = _inproj(x, g_mix, w_qkvu, cs, tm=512)
    _, km, vm, am, bm = meta
    o = _attention(q, k, v, _pad_rows(km, GRID_W), _pad_rows(vm, GRID_W), bias, rows_per_step=16)

    def with_meta(grid_part, meta_part):
        full = jnp.concatenate([jnp.broadcast_to(meta_part[None], (B,) + meta_part.shape), grid_part], axis=1)
        return full.reshape(B, DFT_OUTER, n2 * FN_WIDTH)

    wa, wb, g = _dft_constants(n2)
    z = _dft16(with_meta(a, am), with_meta(b, bm), wa, wb, _column_chunk(n2 * FN_WIDTH))
    f = _dftn2(z.reshape(B, 2 * DFT_OUTER, n2, FN_WIDTH), g).reshape(B, T, FN_WIDTH)
    return _trunk(x, o, f, g_mix, wg, wna, wfn, wout, g_mlp, wup, wdn, g_fin, tm=256)


def kernel(x_prompt, x_sample, meta_tokens, w_in, rel_bias, meta_bias, w_branch_na, w_branch_fn,
           w_out, g_mix, g_mlp, w_up, w_down, g_final):
    assert w_in.shape[0] == 1, "single-layer trunk"
    w_in0 = w_in[0]
    split = 3 * NA_WIDTH + FN_WIDTH
    w_qkvu = w_in0[:, :split].astype(BF16)
    wg = w_in0[:, split:].astype(BF16)
    cs = _channel_dft_matrix()
    bias = _bias_tables(rel_bias[0], meta_bias[0])
    g_mix2 = g_mix[0][None]
    consts = (w_qkvu, cs, bias, g_mix2, wg, w_branch_na[0].astype(BF16), w_branch_fn[0].astype(BF16),
              w_out[0].astype(BF16), g_mlp[0][None], w_up[0].astype(BF16), w_down[0].astype(BF16),
              g_final[None])
    meta = _inproj(meta_tokens[None], g_mix2, w_qkvu, cs, tm=N_META)
    meta = tuple(m[0] for m in meta)
    return (_run_group(x_prompt, meta, consts), _run_group(x_sample, meta, consts))
```

```python
import functools
import math

import jax
import jax.numpy as jnp
import numpy as np
from jax import lax
from jax.experimental import pallas as pl
from jax.experimental.pallas import tpu as pltpu

F32 = jnp.float32
BF16 = jnp.bfloat16

D_MODEL = 1024
N_META = 16
GRID_W = 64
NA_HEADS = 8
NA_HEAD_DIM = 64
NA_WIDTH = NA_HEADS * NA_HEAD_DIM
NA_WIN_ROWS = 8
NA_WIN_COLS = 16
FN_GROUPS = 4
FN_GROUP_DIM = 128
FN_WIDTH = FN_GROUPS * FN_GROUP_DIM
D_FF = 4 * D_MODEL
RMS_EPS = 1e-6

LANES = 128
SUBLANES = 8
HEAD_PAIRS = NA_HEADS // 2
WIN_SLOTS = NA_WIN_ROWS + 2
WIN_KEYS = WIN_SLOTS * GRID_W
LOCAL_KEYS = (NA_WIN_ROWS + 1) * GRID_W
NEG = -1e30
DFT_OUTER = 16
VMEM_LIMIT = 56 * 1024 * 1024


def _rms(x, g):
    ms = jnp.mean(x * x, axis=-1, keepdims=True)
    return x * lax.rsqrt(ms + RMS_EPS) * g


def _inproj_body(x_ref, g_ref, w_ref, q_ref, k_ref, v_ref, u_ref):
    h = _rms(x_ref[0], g_ref[...]).astype(BF16)
    z = jnp.dot(h, w_ref[...], preferred_element_type=F32)
    q_ref[0] = (z[:, :NA_WIDTH] * (NA_HEAD_DIM ** -0.5)).astype(BF16)
    k_ref[0] = z[:, NA_WIDTH:2 * NA_WIDTH].astype(BF16)
    v_ref[0] = z[:, 2 * NA_WIDTH:3 * NA_WIDTH].astype(BF16)
    u_ref[0] = z[:, 3 * NA_WIDTH:]


def _inproj(x, g, w, tm):
    B, T, D = x.shape
    tok = pl.BlockSpec((1, tm, NA_WIDTH), lambda b, i: (b, i, 0))
    return pl.pallas_call(
        _inproj_body,
        grid=(B, T // tm),
        in_specs=[
            pl.BlockSpec((1, tm, D), lambda b, i: (b, i, 0)),
            pl.BlockSpec((1, D), lambda b, i: (0, 0)),
            pl.BlockSpec(w.shape, lambda b, i: (0, 0)),
        ],
        out_specs=[tok] * 4,
        out_shape=[jax.ShapeDtypeStruct((B, T, NA_WIDTH), BF16)] * 3
        + [jax.ShapeDtypeStruct((B, T, FN_WIDTH), F32)],
        compiler_params=pltpu.CompilerParams(
            dimension_semantics=("arbitrary", "arbitrary"), vmem_limit_bytes=VMEM_LIMIT),
        name="inproj",
    )(x, g, w)


def _attn_body(q_ref, k_ref, v_ref, km_ref, vm_ref, bias_ref, o_ref, *, rows, rows_per_step):
    chunk = pl.program_id(2)
    lane = lax.broadcasted_iota(jnp.int32, (2 * GRID_W, LANES), 1)
    first_head = lane < NA_HEAD_DIM
    km = km_ref[...]
    vm = vm_ref[...]
    zero = jnp.zeros((2 * GRID_W, LANES), BF16)

    def pair(i, carry):
        r = chunk * rows_per_step + 2 * i
        start = jnp.clip(r - NA_WIN_ROWS // 2, 0, rows - (NA_WIN_ROWS + 1))
        cfg = jnp.where(r == 0, 0, jnp.where(r == 2, 1, jnp.where(
            r == rows - 4, 3, jnp.where(r == rows - 2, 4, 2))))
        q2 = q_ref[0, pl.ds(pl.multiple_of(i * 2 * GRID_W, 2 * GRID_W), 2 * GRID_W), :]
        qq = jnp.concatenate([jnp.where(first_head, q2, zero), jnp.where(first_head, zero, q2)], axis=0)
        koff = pl.multiple_of(start * GRID_W, GRID_W)
        kw = jnp.concatenate([k_ref[0, pl.ds(koff, LOCAL_KEYS), :], km], axis=0)
        vw = jnp.concatenate([v_ref[0, pl.ds(koff, LOCAL_KEYS), :], vm], axis=0)
        s = lax.dot_general(qq, kw, (((1,), (1,)), ((), ())), preferred_element_type=F32)
        s = s + bias_ref[0, cfg]
        m = jnp.max(s, axis=-1, keepdims=True)
        p = jnp.exp(s - m)
        l = jnp.sum(p, axis=-1, keepdims=True)
        o = jnp.dot(p.astype(BF16), vw, preferred_element_type=F32) / l
        o2 = jnp.where(first_head, o[:2 * GRID_W], o[2 * GRID_W:])
        o_ref[0, pl.ds(pl.multiple_of(i * 2 * GRID_W, 2 * GRID_W), 2 * GRID_W), :] = o2.astype(BF16)
        return carry

    lax.fori_loop(0, rows_per_step // 2, pair, 0)


def _attention(q, k, v, km, vm, bias, rows_per_step):
    B, T, _ = q.shape
    rows = T // GRID_W
    tq = rows_per_step * GRID_W
    body = functools.partial(_attn_body, rows=rows, rows_per_step=rows_per_step)
    return pl.pallas_call(
        body,
        grid=(B, HEAD_PAIRS, rows // rows_per_step),
        in_specs=[
            pl.BlockSpec((1, tq, LANES), lambda b, h, c: (b, c, h)),
            pl.BlockSpec((1, T, LANES), lambda b, h, c: (b, 0, h)),
            pl.BlockSpec((1, T, LANES), lambda b, h, c: (b, 0, h)),
            pl.BlockSpec((GRID_W, LANES), lambda b, h, c: (0, h)),
            pl.BlockSpec((GRID_W, LANES), lambda b, h, c: (0, h)),
            pl.BlockSpec((1,) + bias.shape[1:], lambda b, h, c: (h, 0, 0, 0)),
        ],
        out_specs=pl.BlockSpec((1, tq, LANES), lambda b, h, c: (b, c, h)),
        out_shape=jax.ShapeDtypeStruct((B, T, NA_WIDTH), BF16),
        compiler_params=pltpu.CompilerParams(
            dimension_semantics=("arbitrary", "arbitrary", "arbitrary"), vmem_limit_bytes=VMEM_LIMIT),
        name="attention",
    )(q, k, v, km, vm, bias)


def _bias_tables(rel_bias, meta_bias):
    c = np.arange(GRID_W)[:, None]
    j = np.arange(GRID_W)[None, :]
    cs = np.clip(c - NA_WIN_COLS // 2, 0, GRID_W - NA_WIN_COLS)
    valid = (j >= cs) & (j < cs + NA_WIN_COLS)
    co = np.clip(j - c + NA_WIN_COLS - 1, 0, 2 * NA_WIN_COLS - 2)
    toe = jnp.where(valid[None, None], rel_bias[:, :, co], NEG)
    negblk = jnp.full((NA_HEADS, GRID_W, GRID_W), NEG, F32)
    metablk = jnp.concatenate(
        [jnp.broadcast_to(meta_bias[:, None, :], (NA_HEADS, GRID_W, N_META)),
         jnp.full((NA_HEADS, GRID_W, GRID_W - N_META), NEG, F32)], axis=-1)
    cfgs = [((0, 1), ((0, 8), (0, 8))), ((2, 3), ((0, 8), (0, 8))), ((4, 5), ((0, 8), (1, 9))),
            ((5, 6), ((1, 9), (1, 9))), ((7, 8), ((1, 9), (1, 9)))]
    per_cfg = []
    for qslots, wins in cfgs:
        per_row = []
        for a, (w0, w1) in zip(qslots, wins):
            blocks = [toe[:, w - a + NA_WIN_ROWS - 1] if w0 <= w < w1 else negblk
                      for w in range(NA_WIN_ROWS + 1)]
            per_row.append(jnp.concatenate(blocks + [metablk], axis=-1))
        per_cfg.append(jnp.concatenate(per_row, axis=1))
    t = jnp.stack(per_cfg, axis=1)
    t = t.reshape(HEAD_PAIRS, 2, len(cfgs), 2 * GRID_W, WIN_KEYS)
    return jnp.transpose(t, (0, 2, 1, 3, 4)).reshape(HEAD_PAIRS, len(cfgs), 4 * GRID_W, WIN_KEYS)


def _snap(x):
    for v in (0.0, 1.0, -1.0):
        if abs(x - v) < 1e-12:
            return v
    return x


def _scaled(x, coef):
    if x is None or coef == 0.0:
        return None
    if coef == 1.0:
        return x
    if coef == -1.0:
        return -x
    return x * coef


def _plus(a, b):
    if a is None:
        return b
    if b is None:
        return a
    return a + b


def _minus(a, b):
    if b is None:
        return a
    if a is None:
        return -b
    return a - b


def _fft_dit(zs, keep=None):
    n = len(zs)
    if n == 1:
        return zs
    even, odd = _fft_dit(zs[0::2]), _fft_dit(zs[1::2])
    out = [None] * n
    for kk in range(n // 2):
        c, s = _snap(math.cos(2 * math.pi * kk / n)), _snap(math.sin(2 * math.pi * kk / n))
        orr, oi = odd[kk]
        tr = _plus(_scaled(orr, c), _scaled(oi, s))
        ti = _plus(_scaled(oi, c), _scaled(orr, -s))
        er, ei = even[kk]
        if keep is None or kk in keep:
            out[kk] = (_plus(er, tr), _plus(ei, ti))
        if keep is None or kk + n // 2 in keep:
            out[kk + n // 2] = (_minus(er, tr), _minus(ei, ti))
    return out


def _dft16_real(xs):
    half = DFT_OUTER // 2
    return _fft_dit([(x, None) for x in xs], keep=set(range(half + 1)))[:half + 1]


def _fourier_body(u_ref, um_ref, g_ref, gt_ref, twc_ref, tws_ref, cs_ref, f_ref,
                  xs_ref, zr_ref, zi_ref, *, n2):
    nm = n2 - 1
    half = DFT_OUTER // 2
    xs_ref[0, 0:N_META] = um_ref[...]
    xs_ref[0, N_META:nm] = u_ref[0, 0:nm - N_META]
    for a in range(1, DFT_OUTER):
        xs_ref[a] = u_ref[0, pl.ds(n2 * a - N_META, nm), :]
    tails = [u_ref[0, pl.ds(n2 * a + nm - N_META, 1), :] for a in range(DFT_OUTER)]

    def stage1(i, carry):
        r0 = pl.multiple_of(i * SUBLANES, SUBLANES)
        zs = _dft16_real([xs_ref[a, pl.ds(r0, SUBLANES), :] for a in range(DFT_OUTER)])
        for c, (zr, zi) in enumerate(zs):
            zr_ref[c, pl.ds(r0, SUBLANES), :] = zr
            if zi is not None:
                zi_ref[c, pl.ds(r0, SUBLANES), :] = zi
        return carry

    lax.fori_loop(0, nm // SUBLANES, stage1, 0)
    zt = _dft16_real(tails)

    g = g_ref[...]
    gt = gt_ref[...]
    csm = cs_ref[...]
    for c in range(DFT_OUTER):
        src = c if c <= half else DFT_OUTER - c
        conj = c > half
        real_only = src in (0, half)
        zr = zr_ref[src]
        ztr, zti = zt[src]
        zi = None if real_only else zi_ref[src]
        if conj and zi is not None:
            zi, zti = -zi, -zti
        if c > 0:
            tc, ts = twc_ref[0:nm, c:c + 1], tws_ref[0:nm, c:c + 1]
            tct, tst = twc_ref[nm:n2, c:c + 1], tws_ref[nm:n2, c:c + 1]
            if zi is None:
                zr, zi = zr * tc, -(zr * ts)
                ztr, zti = ztr * tct, -(ztr * tst)
            else:
                zr, zi = zr * tc + zi * ts, zi * tc - zr * ts
                ztr, zti = ztr * tct + zti * tst, zti * tct - ztr * tst
        if zi is None:
            zi, zti = jnp.zeros_like(zr), jnp.zeros_like(ztr)
        rhs = jnp.concatenate([zr, zi], axis=1).astype(BF16)
        p = jnp.dot(g, rhs, preferred_element_type=F32)
        p = p + gt * jnp.concatenate([ztr, zti], axis=1)
        pr = p[:nm, :LANES] + p[nm:, LANES:]
        pi = p[:nm, LANES:] - p[nm:, :LANES]
        y = jnp.dot(jnp.concatenate([pr, pi], axis=1).astype(BF16), csm, preferred_element_type=F32)
        f_ref[0, pl.ds(c, nm, stride=DFT_OUTER), :] = y


def _fourier(u, um, g, gt, twc, tws, cs):
    B, T, _ = u.shape
    n2 = (N_META + T) // DFT_OUTER
    nm = n2 - 1
    body = functools.partial(_fourier_body, n2=n2)

    def const(arr):
        return pl.BlockSpec(arr.shape, lambda b, j: (0, 0))

    return pl.pallas_call(
        body,
        grid=(B, FN_GROUPS),
        in_specs=[
            pl.BlockSpec((1, T, LANES), lambda b, j: (b, 0, j)),
            pl.BlockSpec((N_META, LANES), lambda b, j: (0, j)),
            const(g), const(gt), const(twc), const(tws), const(cs),
        ],
        out_specs=pl.BlockSpec((1, T, LANES), lambda b, j: (b, 0, j)),
        out_shape=jax.ShapeDtypeStruct((B, T, FN_WIDTH), F32),
        scratch_shapes=[
            pltpu.VMEM((DFT_OUTER, nm, LANES), F32),
            pltpu.VMEM((DFT_OUTER // 2 + 1, nm, LANES), F32),
            pltpu.VMEM((DFT_OUTER // 2 + 1, nm, LANES), F32),
        ],
        compiler_params=pltpu.CompilerParams(
            dimension_semantics=("arbitrary", "arbitrary"), vmem_limit_bytes=VMEM_LIMIT),
        name="fourier",
    )(u, um, g, gt, twc, tws, cs)


def _fourier_constants(n2):
    n = DFT_OUTER * n2
    b = jnp.arange(n2, dtype=jnp.int32)
    d = jnp.arange(1, n2, dtype=jnp.int32)
    th = ((d[:, None] * b[None, :]) % n2).astype(F32) * (2.0 * math.pi / n2)
    full = jnp.concatenate([jnp.cos(th), jnp.sin(th)], axis=0) * (1.0 / math.sqrt(n))
    g, gt = full[:, :n2 - 1].astype(BF16), full[:, n2 - 1:]
    c = jnp.arange(DFT_OUTER, dtype=jnp.int32)
    tw = ((b[:, None] * c[None, :]) % n).astype(F32) * (2.0 * math.pi / n)
    return g, gt, jnp.cos(tw), jnp.sin(tw)


def _channel_dft_matrix():
    ch = jnp.arange(FN_GROUP_DIM, dtype=jnp.int32)
    th = ((ch[:, None] * ch[None, :]) % FN_GROUP_DIM).astype(F32) * (2.0 * math.pi / FN_GROUP_DIM)
    scale = 1.0 / math.sqrt(FN_GROUP_DIM)
    return jnp.concatenate([jnp.cos(th) * scale, jnp.sin(th) * scale], axis=0).astype(BF16)


def _trunk_body(x_ref, o_ref, f_ref, gmix_ref, wg_ref, wna_ref, wfn_ref, wout_ref, gmlp_ref,
                wup_ref, wdn_ref, gfin_ref, y_ref, *, ff_chunk):
    x = x_ref[0]
    h = _rms(x, gmix_ref[...]).astype(BF16)
    gates = jnp.dot(h, wg_ref[...], preferred_element_type=F32)
    y_na = jnp.dot(o_ref[0], wna_ref[...], preferred_element_type=F32)
    y_fn = jnp.dot(f_ref[0].astype(BF16), wfn_ref[...], preferred_element_type=F32)
    mixed = jax.nn.sigmoid(gates[:, :D_MODEL]) * y_na + jax.nn.sigmoid(gates[:, D_MODEL:]) * y_fn
    x1 = x + jnp.dot(mixed.astype(BF16), wout_ref[...], preferred_element_type=F32)
    h2 = _rms(x1, gmlp_ref[...]).astype(BF16)
    acc = x1
    for c in range(D_FF // ff_chunk):
        up = jnp.dot(h2, wup_ref[:, c * ff_chunk:(c + 1) * ff_chunk], preferred_element_type=F32)
        up = jnp.maximum(up, 0.0)
        acc = acc + jnp.dot((up * up).astype(BF16), wdn_ref[c * ff_chunk:(c + 1) * ff_chunk, :],
                            preferred_element_type=F32)
    y_ref[0] = _rms(acc, gfin_ref[...])


def _trunk(x, o, f, gmix, wg, wna, wfn, wout, gmlp, wup, wdn, gfin, tm):
    B, T, D = x.shape
    body = functools.partial(_trunk_body, ff_chunk=1024)

    def const(arr):
        return pl.BlockSpec(arr.shape, lambda b, i: (0, 0), pipeline_mode=pl.Buffered(1))

    return pl.pallas_call(
        body,
        grid=(B, T // tm),
        in_specs=[
            pl.BlockSpec((1, tm, D), lambda b, i: (b, i, 0)),
            pl.BlockSpec((1, tm, NA_WIDTH), lambda b, i: (b, i, 0)),
            pl.BlockSpec((1, tm, FN_WIDTH), lambda b, i: (b, i, 0)),
            const(gmix), const(wg), const(wna), const(wfn), const(wout), const(gmlp),
            const(wup), const(wdn), const(gfin),
        ],
        out_specs=pl.BlockSpec((1, tm, D), lambda b, i: (b, i, 0)),
        out_shape=jax.ShapeDtypeStruct((B, T, D), F32),
        compiler_params=pltpu.CompilerParams(
            dimension_semantics=("arbitrary", "arbitrary"), vmem_limit_bytes=VMEM_LIMIT),
        name="trunk",
    )(x, o, f, gmix, wg, wna, wfn, wout, gmlp, wup, wdn, gfin)


def _pad_rows(a, rows):
    return jnp.concatenate([a, jnp.zeros((rows - a.shape[0], a.shape[1]), a.dtype)], axis=0)


def _run_group(x, meta, consts):
    (w_qkvu, cs, bias, g_mix, wg, wna, wfn, wout, g_mlp, wup, wdn, g_fin) = consts
    B, T, _ = x.shape
    n2 = (N_META + T) // DFT_OUTER
    q, k, v, u = _inproj(x, g_mix, w_qkvu, tm=512)
    _, km, vm, um = meta
    o = _attention(q, k, v, _pad_rows(km, GRID_W), _pad_rows(vm, GRID_W), bias, rows_per_step=16)
    f = _fourier(u, um, *_fourier_constants(n2), cs)
    return _trunk(x, o, f, g_mix, wg, wna, wfn, wout, g_mlp, wup, wdn, g_fin, tm=256)


def kernel(x_prompt, x_sample, meta_tokens, w_in, rel_bias, meta_bias, w_branch_na, w_branch_fn,
           w_out, g_mix, g_mlp, w_up, w_down, g_final):
    assert w_in.shape[0] == 1, "single-layer trunk"
    w_in0 = w_in[0]
    split = 3 * NA_WIDTH + FN_WIDTH
    w_qkvu = w_in0[:, :split].astype(BF16)
    wg = w_in0[:, split:].astype(BF16)
    cs = _channel_dft_matrix()
    bias = _bias_tables(rel_bias[0], meta_bias[0])
    g_mix2 = g_mix[0][None]
    consts = (w_qkvu, cs, bias, g_mix2, wg, w_branch_na[0].astype(BF16), w_branch_fn[0].astype(BF16),
              w_out[0].astype(BF16), g_mlp[0][None], w_up[0].astype(BF16), w_down[0].astype(BF16),
              g_final[None])
    meta = _inproj(meta_tokens[None], g_mix2, w_qkvu, tm=N_META)
    meta = tuple(m[0] for m in meta)
    return (_run_group(x_prompt, meta, consts), _run_group(x_sample, meta, consts))
```

```python
import functools
import math

import jax
import jax.numpy as jnp
import numpy as np
from jax import lax
from jax.experimental import pallas as pl
from jax.experimental.pallas import tpu as pltpu

F32 = jnp.float32
BF16 = jnp.bfloat16

D_MODEL = 1024
N_META = 16
GRID_W = 64
NA_HEADS = 8
NA_HEAD_DIM = 64
NA_WIDTH = NA_HEADS * NA_HEAD_DIM
NA_WIN_ROWS = 8
NA_WIN_COLS = 16
FN_GROUPS = 4
FN_GROUP_DIM = 128
FN_WIDTH = FN_GROUPS * FN_GROUP_DIM
D_FF = 4 * D_MODEL
RMS_EPS = 1e-6

LANES = 128
SUBLANES = 8
HEAD_PAIRS = NA_HEADS // 2
WIN_SLOTS = NA_WIN_ROWS + 2
WIN_KEYS = WIN_SLOTS * GRID_W
LOCAL_KEYS = (NA_WIN_ROWS + 1) * GRID_W
NEG = -1e30
LOG2E = math.log2(math.e)
DFT_OUTER = 16
VMEM_LIMIT = 56 * 1024 * 1024


def _rms(x, g):
    ms = jnp.mean(x * x, axis=-1, keepdims=True)
    return x * lax.rsqrt(ms + RMS_EPS) * g


def _inproj_body(x_ref, g_ref, w_ref, q_ref, k_ref, v_ref, u_ref):
    h = _rms(x_ref[0], g_ref[...]).astype(BF16)
    z = jnp.dot(h, w_ref[...], preferred_element_type=F32)
    q_ref[0] = (z[:, :NA_WIDTH] * (NA_HEAD_DIM ** -0.5 * LOG2E)).astype(BF16)
    k_ref[0] = z[:, NA_WIDTH:2 * NA_WIDTH].astype(BF16)
    v_ref[0] = z[:, 2 * NA_WIDTH:3 * NA_WIDTH].astype(BF16)
    u_ref[0] = z[:, 3 * NA_WIDTH:]


def _inproj(x, g, w, tm):
    B, T, D = x.shape
    tok = pl.BlockSpec((1, tm, NA_WIDTH), lambda b, i: (b, i, 0))
    return pl.pallas_call(
        _inproj_body,
        grid=(B, T // tm),
        in_specs=[
            pl.BlockSpec((1, tm, D), lambda b, i: (b, i, 0)),
            pl.BlockSpec((1, D), lambda b, i: (0, 0)),
            pl.BlockSpec(w.shape, lambda b, i: (0, 0)),
        ],
        out_specs=[tok] * 4,
        out_shape=[jax.ShapeDtypeStruct((B, T, NA_WIDTH), BF16)] * 3
        + [jax.ShapeDtypeStruct((B, T, FN_WIDTH), F32)],
        compiler_params=pltpu.CompilerParams(
            dimension_semantics=("arbitrary", "arbitrary"), vmem_limit_bytes=VMEM_LIMIT),
        name="inproj",
    )(x, g, w)


def _attn_body(q_ref, k_ref, v_ref, km_ref, vm_ref, bias_ref, o_ref, *, rows, rows_per_step):
    chunk = pl.program_id(2)
    first_head = lax.broadcasted_iota(jnp.int32, (2 * GRID_W, LANES), 1) < NA_HEAD_DIM
    first_head_kv = lax.broadcasted_iota(jnp.int32, (WIN_KEYS, LANES), 1) < NA_HEAD_DIM
    km = km_ref[...]
    vm = vm_ref[...]
    zero = jnp.zeros((2 * GRID_W, LANES), BF16)
    ones = jnp.ones((WIN_KEYS, LANES), BF16)

    def scores(i):
        r = chunk * rows_per_step + 2 * i
        start = jnp.clip(r - NA_WIN_ROWS // 2, 0, rows - (NA_WIN_ROWS + 1))
        cfg = jnp.where(r == 0, 0, jnp.where(r == 2, 1, jnp.where(
            r == rows - 4, 3, jnp.where(r == rows - 2, 4, 2))))
        q2 = q_ref[0, i * 2 * GRID_W:(i + 1) * 2 * GRID_W, :]
        qq = jnp.concatenate([jnp.where(first_head, q2, zero), jnp.where(first_head, zero, q2)], axis=0)
        koff = pl.multiple_of(start * GRID_W, GRID_W)
        kw = jnp.concatenate([k_ref[0, pl.ds(koff, LOCAL_KEYS), :], km], axis=0)
        s = lax.dot_general(qq, kw, (((1,), (1,)), ((), ())), preferred_element_type=F32)
        return s + bias_ref[0, cfg], koff

    def finish(i, s, koff):
        vw = jnp.concatenate([v_ref[0, pl.ds(koff, LOCAL_KEYS), :], vm], axis=0)
        v0 = jnp.where(first_head_kv, vw, ones)
        v1 = jnp.where(first_head_kv, ones, vw)
        m = jnp.max(s, axis=-1, keepdims=True)
        p = jnp.exp2(s - m).astype(BF16)
        o0 = jnp.dot(p[:2 * GRID_W], v0, preferred_element_type=F32)
        o1 = jnp.dot(p[2 * GRID_W:], v1, preferred_element_type=F32)
        r0 = o0 / pltpu.roll(o0, NA_HEAD_DIM, axis=1)
        r1 = o1 / pltpu.roll(o1, NA_HEAD_DIM, axis=1)
        o_ref[0, i * 2 * GRID_W:(i + 1) * 2 * GRID_W, :] = jnp.where(first_head, r0, r1).astype(BF16)

    n_pairs = rows_per_step // 2
    pending = scores(0)
    for i in range(n_pairs):
        nxt = scores(i + 1) if i + 1 < n_pairs else None
        finish(i, *pending)
        pending = nxt


def _attention(q, k, v, km, vm, bias, rows_per_step):
    B, T, _ = q.shape
    rows = T // GRID_W
    tq = rows_per_step * GRID_W
    body = functools.partial(_attn_body, rows=rows, rows_per_step=rows_per_step)
    return pl.pallas_call(
        body,
        grid=(B, HEAD_PAIRS, rows // rows_per_step),
        in_specs=[
            pl.BlockSpec((1, tq, LANES), lambda b, h, c: (b, c, h)),
            pl.BlockSpec((1, T, LANES), lambda b, h, c: (b, 0, h)),
            pl.BlockSpec((1, T, LANES), lambda b, h, c: (b, 0, h)),
            pl.BlockSpec((GRID_W, LANES), lambda b, h, c: (0, h)),
            pl.BlockSpec((GRID_W, LANES), lambda b, h, c: (0, h)),
            pl.BlockSpec((1,) + bias.shape[1:], lambda b, h, c: (h, 0, 0, 0)),
        ],
        out_specs=pl.BlockSpec((1, tq, LANES), lambda b, h, c: (b, c, h)),
        out_shape=jax.ShapeDtypeStruct((B, T, NA_WIDTH), BF16),
        compiler_params=pltpu.CompilerParams(
            dimension_semantics=("arbitrary", "arbitrary", "arbitrary"), vmem_limit_bytes=VMEM_LIMIT),
        name="attention",
    )(q, k, v, km, vm, bias)


def _bias_tables(rel_bias, meta_bias):
    c = np.arange(GRID_W)[:, None]
    j = np.arange(GRID_W)[None, :]
    cs = np.clip(c - NA_WIN_COLS // 2, 0, GRID_W - NA_WIN_COLS)
    valid = (j >= cs) & (j < cs + NA_WIN_COLS)
    co = np.clip(j - c + NA_WIN_COLS - 1, 0, 2 * NA_WIN_COLS - 2)
    toe = jnp.where(valid[None, None], rel_bias[:, :, co], NEG)
    negblk = jnp.full((NA_HEADS, GRID_W, GRID_W), NEG, F32)
    metablk = jnp.concatenate(
        [jnp.broadcast_to(meta_bias[:, None, :], (NA_HEADS, GRID_W, N_META)),
         jnp.full((NA_HEADS, GRID_W, GRID_W - N_META), NEG, F32)], axis=-1)
    cfgs = [((0, 1), ((0, 8), (0, 8))), ((2, 3), ((0, 8), (0, 8))), ((4, 5), ((0, 8), (1, 9))),
            ((5, 6), ((1, 9), (1, 9))), ((7, 8), ((1, 9), (1, 9)))]
    per_cfg = []
    for qslots, wins in cfgs:
        per_row = []
        for a, (w0, w1) in zip(qslots, wins):
            blocks = [toe[:, w - a + NA_WIN_ROWS - 1] if w0 <= w < w1 else negblk
                      for w in range(NA_WIN_ROWS + 1)]
            per_row.append(jnp.concatenate(blocks + [metablk], axis=-1))
        per_cfg.append(jnp.concatenate(per_row, axis=1))
    t = jnp.stack(per_cfg, axis=1) * LOG2E
    t = t.reshape(HEAD_PAIRS, 2, len(cfgs), 2 * GRID_W, WIN_KEYS)
    return jnp.transpose(t, (0, 2, 1, 3, 4)).reshape(HEAD_PAIRS, len(cfgs), 4 * GRID_W, WIN_KEYS)


def _snap(x):
    for v in (0.0, 1.0, -1.0):
        if abs(x - v) < 1e-12:
            return v
    return x


def _scaled(x, coef):
    if x is None or coef == 0.0:
        return None
    if coef == 1.0:
        return x
    if coef == -1.0:
        return -x
    return x * coef


def _plus(a, b):
    if a is None:
        return b
    if b is None:
        return a
    return a + b


def _minus(a, b):
    if b is None:
        return a
    if a is None:
        return -b
    return a - b


def _fft_dit(zs, keep=None):
    n = len(zs)
    if n == 1:
        return zs
    even, odd = _fft_dit(zs[0::2]), _fft_dit(zs[1::2])
    out = [None] * n
    for kk in range(n // 2):
        c, s = _snap(math.cos(2 * math.pi * kk / n)), _snap(math.sin(2 * math.pi * kk / n))
        orr, oi = odd[kk]
        tr = _plus(_scaled(orr, c), _scaled(oi, s))
        ti = _plus(_scaled(oi, c), _scaled(orr, -s))
        er, ei = even[kk]
        if keep is None or kk in keep:
            out[kk] = (_plus(er, tr), _plus(ei, ti))
        if keep is None or kk + n // 2 in keep:
            out[kk + n // 2] = (_minus(er, tr), _minus(ei, ti))
    return out


def _dft16_real(xs):
    half = DFT_OUTER // 2
    return _fft_dit([(x, None) for x in xs], keep=set(range(half + 1)))[:half + 1]


def _fourier_body(u_ref, um_ref, g_ref, gt_ref, twc_ref, tws_ref, cs_ref, f_ref,
                  xs_ref, zr_ref, zi_ref, *, n2):
    nm = n2 - 1
    half = DFT_OUTER // 2
    xs_ref[0, 0:N_META] = um_ref[...]
    xs_ref[0, N_META:nm] = u_ref[0, 0:nm - N_META]
    for a in range(1, DFT_OUTER):
        xs_ref[a] = u_ref[0, pl.ds(n2 * a - N_META, nm), :]
    tails = [u_ref[0, pl.ds(n2 * a + nm - N_META, 1), :] for a in range(DFT_OUTER)]

    def stage1(i, carry):
        r0 = pl.multiple_of(i * SUBLANES, SUBLANES)
        zs = _dft16_real([xs_ref[a, pl.ds(r0, SUBLANES), :] for a in range(DFT_OUTER)])
        for c, (zr, zi) in enumerate(zs):
            zr_ref[c, pl.ds(r0, SUBLANES), :] = zr
            if zi is not None:
                zi_ref[c, pl.ds(r0, SUBLANES), :] = zi
        return carry

    lax.fori_loop(0, nm // SUBLANES, stage1, 0)
    zt = _dft16_real(tails)

    g = g_ref[...]
    gt = gt_ref[...]
    csm = cs_ref[...]
    for c in range(DFT_OUTER):
        src = c if c <= half else DFT_OUTER - c
        conj = c > half
        real_only = src in (0, half)
        zr = zr_ref[src]
        ztr, zti = zt[src]
        zi = None if real_only else zi_ref[src]
        if conj and zi is not None:
            zi, zti = -zi, -zti
        if c > 0:
            tc, ts = twc_ref[0:nm, c:c + 1], tws_ref[0:nm, c:c + 1]
            tct, tst = twc_ref[nm:n2, c:c + 1], tws_ref[nm:n2, c:c + 1]
            if zi is None:
                zr, zi = zr * tc, -(zr * ts)
                ztr, zti = ztr * tct, -(ztr * tst)
            else:
                zr, zi = zr * tc + zi * ts, zi * tc - zr * ts
                ztr, zti = ztr * tct + zti * tst, zti * tct - ztr * tst
        if zi is None:
            zi, zti = jnp.zeros_like(zr), jnp.zeros_like(ztr)
        rhs = jnp.concatenate([zr, zi], axis=1).astype(BF16)
        p = jnp.dot(g, rhs, preferred_element_type=F32)
        p = p + gt * jnp.concatenate([ztr, zti], axis=1)
        pr = p[:nm, :LANES] + p[nm:, LANES:]
        pi = p[:nm, LANES:] - p[nm:, :LANES]
        y = jnp.dot(jnp.concatenate([pr, pi], axis=1).astype(BF16), csm, preferred_element_type=F32)
        f_ref[0, pl.ds(c, nm, stride=DFT_OUTER), :] = y


def _fourier(u, um, g, gt, twc, tws, cs):
    B, T, _ = u.shape
    n2 = (N_META + T) // DFT_OUTER
    nm = n2 - 1
    body = functools.partial(_fourier_body, n2=n2)

    def const(arr):
        return pl.BlockSpec(arr.shape, lambda b, j: (0, 0))

    return pl.pallas_call(
        body,
        grid=(B, FN_GROUPS),
        in_specs=[
            pl.BlockSpec((1, T, LANES), lambda b, j: (b, 0, j)),
            pl.BlockSpec((N_META, LANES), lambda b, j: (0, j)),
            const(g), const(gt), const(twc), const(tws), const(cs),
        ],
        out_specs=pl.BlockSpec((1, T, LANES), lambda b, j: (b, 0, j)),
        out_shape=jax.ShapeDtypeStruct((B, T, FN_WIDTH), F32),
        scratch_shapes=[
            pltpu.VMEM((DFT_OUTER, nm, LANES), F32),
            pltpu.VMEM((DFT_OUTER // 2 + 1, nm, LANES), F32),
            pltpu.VMEM((DFT_OUTER // 2 + 1, nm, LANES), F32),
        ],
        compiler_params=pltpu.CompilerParams(
            dimension_semantics=("arbitrary", "arbitrary"), vmem_limit_bytes=VMEM_LIMIT),
        name="fourier",
    )(u, um, g, gt, twc, tws, cs)


def _fourier_constants(n2):
    n = DFT_OUTER * n2
    b = jnp.arange(n2, dtype=jnp.int32)
    d = jnp.arange(1, n2, dtype=jnp.int32)
    th = ((d[:, None] * b[None, :]) % n2).astype(F32) * (2.0 * math.pi / n2)
    full = jnp.concatenate([jnp.cos(th), jnp.sin(th)], axis=0) * (1.0 / math.sqrt(n))
    g, gt = full[:, :n2 - 1].astype(BF16), full[:, n2 - 1:]
    c = jnp.arange(DFT_OUTER, dtype=jnp.int32)
    tw = ((b[:, None] * c[None, :]) % n).astype(F32) * (2.0 * math.pi / n)
    return g, gt, jnp.cos(tw), jnp.sin(tw)


def _channel_dft_matrix():
    ch = jnp.arange(FN_GROUP_DIM, dtype=jnp.int32)
    th = ((ch[:, None] * ch[None, :]) % FN_GROUP_DIM).astype(F32) * (2.0 * math.pi / FN_GROUP_DIM)
    scale = 1.0 / math.sqrt(FN_GROUP_DIM)
    return jnp.concatenate([jnp.cos(th) * scale, jnp.sin(th) * scale], axis=0).astype(BF16)


def _trunk_body(x_ref, o_ref, f_ref, gmix_ref, wg_ref, wna_ref, wfn_ref, wout_ref, gmlp_ref,
                wup_ref, wdn_ref, gfin_ref, y_ref, *, ff_chunk):
    x = x_ref[0]
    h = _rms(x, gmix_ref[...]).astype(BF16)
    gates = jnp.dot(h, wg_ref[...], preferred_element_type=F32)
    y_na = jnp.dot(o_ref[0], wna_ref[...], preferred_element_type=F32)
    y_fn = jnp.dot(f_ref[0].astype(BF16), wfn_ref[...], preferred_element_type=F32)
    mixed = jax.nn.sigmoid(gates[:, :D_MODEL]) * y_na + jax.nn.sigmoid(gates[:, D_MODEL:]) * y_fn
    x1 = x + jnp.dot(mixed.astype(BF16), wout_ref[...], preferred_element_type=F32)
    h2 = _rms(x1, gmlp_ref[...]).astype(BF16)
    acc = x1
    for c in range(D_FF // ff_chunk):
        up = jnp.dot(h2, wup_ref[:, c * ff_chunk:(c + 1) * ff_chunk], preferred_element_type=F32)
        up = jnp.maximum(up, 0.0)
        acc = acc + jnp.dot((up * up).astype(BF16), wdn_ref[c * ff_chunk:(c + 1) * ff_chunk, :],
                            preferred_element_type=F32)
    y_ref[0] = _rms(acc, gfin_ref[...])


def _trunk(x, o, f, gmix, wg, wna, wfn, wout, gmlp, wup, wdn, gfin, tm):
    B, T, D = x.shape
    body = functools.partial(_trunk_body, ff_chunk=1024)

    def const(arr):
        return pl.BlockSpec(arr.shape, lambda b, i: (0, 0), pipeline_mode=pl.Buffered(1))

    return pl.pallas_call(
        body,
        grid=(B, T // tm),
        in_specs=[
            pl.BlockSpec((1, tm, D), lambda b, i: (b, i, 0)),
            pl.BlockSpec((1, tm, NA_WIDTH), lambda b, i: (b, i, 0)),
            pl.BlockSpec((1, tm, FN_WIDTH), lambda b, i: (b, i, 0)),
            const(gmix), const(wg), const(wna), const(wfn), const(wout), const(gmlp),
            const(wup), const(wdn), const(gfin),
        ],
        out_specs=pl.BlockSpec((1, tm, D), lambda b, i: (b, i, 0)),
        out_shape=jax.ShapeDtypeStruct((B, T, D), F32),
        compiler_params=pltpu.CompilerParams(
            dimension_semantics=("arbitrary", "arbitrary"), vmem_limit_bytes=VMEM_LIMIT),
        name="trunk",
    )(x, o, f, gmix, wg, wna, wfn, wout, gmlp, wup, wdn, gfin)


def _pad_rows(a, rows):
    return jnp.concatenate([a, jnp.zeros((rows - a.shape[0], a.shape[1]), a.dtype)], axis=0)


def _run_group(x, meta, consts):
    (w_qkvu, cs, bias, g_mix, wg, wna, wfn, wout, g_mlp, wup, wdn, g_fin) = consts
    B, T, _ = x.shape
    n2 = (N_META + T) // DFT_OUTER
    q, k, v, u = _inproj(x, g_mix, w_qkvu, tm=512)
    _, km, vm, um = meta
    o = _attention(q, k, v, _pad_rows(km, GRID_W), _pad_rows(vm, GRID_W), bias, rows_per_step=32)
    f = _fourier(u, um, *_fourier_constants(n2), cs)
    return _trunk(x, o, f, g_mix, wg, wna, wfn, wout, g_mlp, wup, wdn, g_fin, tm=256)


def kernel(x_prompt, x_sample, meta_tokens, w_in, rel_bias, meta_bias, w_branch_na, w_branch_fn,
           w_out, g_mix, g_mlp, w_up, w_down, g_final):
    assert w_in.shape[0] == 1, "single-layer trunk"
    w_in0 = w_in[0]
    split = 3 * NA_WIDTH + FN_WIDTH
    w_qkvu = w_in0[:, :split].astype(BF16)
    wg = w_in0[:, split:].astype(BF16)
    cs = _channel_dft_matrix()
    bias = _bias_tables(rel_bias[0], meta_bias[0])
    g_mix2 = g_mix[0][None]
    consts = (w_qkvu, cs, bias, g_mix2, wg, w_branch_na[0].astype(BF16), w_branch_fn[0].astype(BF16),
              w_out[0].astype(BF16), g_mlp[0][None], w_up[0].astype(BF16), w_down[0].astype(BF16),
              g_final[None])
    meta = _inproj(meta_tokens[None], g_mix2, w_qkvu, tm=N_META)
    meta = tuple(m[0] for m in meta)
    return (_run_group(x_prompt, meta, consts), _run_group(x_sample, meta, consts))
```

```python
import functools
import math

import jax
import jax.numpy as jnp
import numpy as np
from jax import lax
from jax.experimental import pallas as pl
from jax.experimental.pallas import tpu as pltpu

F32 = jnp.float32
BF16 = jnp.bfloat16

D_MODEL = 1024
N_META = 16
GRID_W = 64
NA_HEADS = 8
NA_HEAD_DIM = 64
NA_WIDTH = NA_HEADS * NA_HEAD_DIM
NA_WIN_ROWS = 8
NA_WIN_COLS = 16
FN_GROUPS = 4
FN_GROUP_DIM = 128
FN_WIDTH = FN_GROUPS * FN_GROUP_DIM
D_FF = 4 * D_MODEL
RMS_EPS = 1e-6

LANES = 128
SUBLANES = 8
HEAD_PAIRS = NA_HEADS // 2
WIN_SLOTS = NA_WIN_ROWS + 2
WIN_KEYS = WIN_SLOTS * GRID_W
LOCAL_KEYS = (NA_WIN_ROWS + 1) * GRID_W
NEG = -1e30
LOG2E = math.log2(math.e)
DFT_OUTER = 16
VMEM_LIMIT = 56 * 1024 * 1024


def _rms(x, g):
    ms = jnp.mean(x * x, axis=-1, keepdims=True)
    return x * lax.rsqrt(ms + RMS_EPS) * g


def _inproj_body(x_ref, g_ref, w_ref, q_ref, k_ref, v_ref, u_ref):
    h = _rms(x_ref[0], g_ref[...]).astype(BF16)
    z = jnp.dot(h, w_ref[...], preferred_element_type=F32)
    q_ref[0] = (z[:, :NA_WIDTH] * (NA_HEAD_DIM ** -0.5 * LOG2E)).astype(BF16)
    k_ref[0] = z[:, NA_WIDTH:2 * NA_WIDTH].astype(BF16)
    v_ref[0] = z[:, 2 * NA_WIDTH:3 * NA_WIDTH].astype(BF16)
    u_ref[0] = z[:, 3 * NA_WIDTH:]


def _inproj(x, g, w, tm):
    B, T, D = x.shape
    tok = pl.BlockSpec((1, tm, NA_WIDTH), lambda b, i: (b, i, 0))
    return pl.pallas_call(
        _inproj_body,
        grid=(B, T // tm),
        in_specs=[
            pl.BlockSpec((1, tm, D), lambda b, i: (b, i, 0)),
            pl.BlockSpec((1, D), lambda b, i: (0, 0)),
            pl.BlockSpec(w.shape, lambda b, i: (0, 0)),
        ],
        out_specs=[tok] * 4,
        out_shape=[jax.ShapeDtypeStruct((B, T, NA_WIDTH), BF16)] * 3
        + [jax.ShapeDtypeStruct((B, T, FN_WIDTH), F32)],
        compiler_params=pltpu.CompilerParams(
            dimension_semantics=("arbitrary", "arbitrary"), vmem_limit_bytes=VMEM_LIMIT),
        name="inproj",
    )(x, g, w)


def _attn_body(q_ref, k_ref, v_ref, km_ref, vm_ref, bias_ref, o_ref, *, rows, rows_per_step):
    chunk = pl.program_id(2)
    first_head = lax.broadcasted_iota(jnp.int32, (2 * GRID_W, LANES), 1) < NA_HEAD_DIM
    first_head_kv = lax.broadcasted_iota(jnp.int32, (WIN_KEYS, LANES), 1) < NA_HEAD_DIM
    km = km_ref[...]
    vm = vm_ref[...]
    zero = jnp.zeros((2 * GRID_W, LANES), BF16)
    ones = jnp.ones((WIN_KEYS, LANES), BF16)

    def scores(i):
        r = chunk * rows_per_step + 2 * i
        start = jnp.clip(r - NA_WIN_ROWS // 2, 0, rows - (NA_WIN_ROWS + 1))
        cfg = jnp.where(r == 0, 0, jnp.where(r == 2, 1, jnp.where(
            r == rows - 4, 3, jnp.where(r == rows - 2, 4, 2))))
        q2 = q_ref[0, i * 2 * GRID_W:(i + 1) * 2 * GRID_W, :]
        qq = jnp.concatenate([jnp.where(first_head, q2, zero), jnp.where(first_head, zero, q2)], axis=0)
        koff = pl.multiple_of(start * GRID_W, GRID_W)
        kw = jnp.concatenate([k_ref[0, pl.ds(koff, LOCAL_KEYS), :], km], axis=0)
        s = lax.dot_general(qq, kw, (((1,), (1,)), ((), ())), preferred_element_type=F32)
        return s + bias_ref[0, cfg], koff

    def finish(i, s, koff):
        vw = jnp.concatenate([v_ref[0, pl.ds(koff, LOCAL_KEYS), :], vm], axis=0)
        v0 = jnp.where(first_head_kv, vw, ones)
        v1 = jnp.where(first_head_kv, ones, vw)
        m = jnp.max(s, axis=-1, keepdims=True)
        p = jnp.exp2(s - m).astype(BF16)
        o0 = jnp.dot(p[:2 * GRID_W], v0, preferred_element_type=F32)
        o1 = jnp.dot(p[2 * GRID_W:], v1, preferred_element_type=F32)
        r0 = o0 / pltpu.roll(o0, NA_HEAD_DIM, axis=1)
        r1 = o1 / pltpu.roll(o1, NA_HEAD_DIM, axis=1)
        o_ref[0, i * 2 * GRID_W:(i + 1) * 2 * GRID_W, :] = jnp.where(first_head, r0, r1).astype(BF16)

    n_pairs = rows_per_step // 2
    pending = scores(0)
    for i in range(n_pairs):
        nxt = scores(i + 1) if i + 1 < n_pairs else None
        finish(i, *pending)
        pending = nxt


def _attention(q, k, v, km, vm, bias, rows_per_step):
    B, T, _ = q.shape
    rows = T // GRID_W
    tq = rows_per_step * GRID_W
    body = functools.partial(_attn_body, rows=rows, rows_per_step=rows_per_step)
    return pl.pallas_call(
        body,
        grid=(B, HEAD_PAIRS, rows // rows_per_step),
        in_specs=[
            pl.BlockSpec((1, tq, LANES), lambda b, h, c: (b, c, h)),
            pl.BlockSpec((1, T, LANES), lambda b, h, c: (b, 0, h)),
            pl.BlockSpec((1, T, LANES), lambda b, h, c: (b, 0, h)),
            pl.BlockSpec((GRID_W, LANES), lambda b, h, c: (0, h)),
            pl.BlockSpec((GRID_W, LANES), lambda b, h, c: (0, h)),
            pl.BlockSpec((1,) + bias.shape[1:], lambda b, h, c: (h, 0, 0, 0)),
        ],
        out_specs=pl.BlockSpec((1, tq, LANES), lambda b, h, c: (b, c, h)),
        out_shape=jax.ShapeDtypeStruct((B, T, NA_WIDTH), BF16),
        compiler_params=pltpu.CompilerParams(
            dimension_semantics=("arbitrary", "arbitrary", "arbitrary"), vmem_limit_bytes=VMEM_LIMIT),
        name="attention",
    )(q, k, v, km, vm, bias)


def _bias_tables(rel_bias, meta_bias):
    c = np.arange(GRID_W)[:, None]
    j = np.arange(GRID_W)[None, :]
    cs = np.clip(c - NA_WIN_COLS // 2, 0, GRID_W - NA_WIN_COLS)
    valid = (j >= cs) & (j < cs + NA_WIN_COLS)
    co = np.clip(j - c + NA_WIN_COLS - 1, 0, 2 * NA_WIN_COLS - 2)
    toe = jnp.where(valid[None, None], rel_bias[:, :, co], NEG)
    negblk = jnp.full((NA_HEADS, GRID_W, GRID_W), NEG, F32)
    metablk = jnp.concatenate(
        [jnp.broadcast_to(meta_bias[:, None, :], (NA_HEADS, GRID_W, N_META)),
         jnp.full((NA_HEADS, GRID_W, GRID_W - N_META), NEG, F32)], axis=-1)
    cfgs = [((0, 1), ((0, 8), (0, 8))), ((2, 3), ((0, 8), (0, 8))), ((4, 5), ((0, 8), (1, 9))),
            ((5, 6), ((1, 9), (1, 9))), ((7, 8), ((1, 9), (1, 9)))]
    per_cfg = []
    for qslots, wins in cfgs:
        per_row = []
        for a, (w0, w1) in zip(qslots, wins):
            blocks = [toe[:, w - a + NA_WIN_ROWS - 1] if w0 <= w < w1 else negblk
                      for w in range(NA_WIN_ROWS + 1)]
            per_row.append(jnp.concatenate(blocks + [metablk], axis=-1))
        per_cfg.append(jnp.concatenate(per_row, axis=1))
    t = jnp.stack(per_cfg, axis=1) * LOG2E
    t = t.reshape(HEAD_PAIRS, 2, len(cfgs), 2 * GRID_W, WIN_KEYS)
    return jnp.transpose(t, (0, 2, 1, 3, 4)).reshape(HEAD_PAIRS, len(cfgs), 4 * GRID_W, WIN_KEYS)


def _snap(x):
    for v in (0.0, 1.0, -1.0):
        if abs(x - v) < 1e-12:
            return v
    return x


def _scaled(x, coef):
    if x is None or coef == 0.0:
        return None
    if coef == 1.0:
        return x
    if coef == -1.0:
        return -x
    return x * coef


def _plus(a, b):
    if a is None:
        return b
    if b is None:
        return a
    return a + b


def _minus(a, b):
    if b is None:
        return a
    if a is None:
        return -b
    return a - b


def _fft_dit(zs, keep=None):
    n = len(zs)
    if n == 1:
        return zs
    even, odd = _fft_dit(zs[0::2]), _fft_dit(zs[1::2])
    out = [None] * n
    for kk in range(n // 2):
        c, s = _snap(math.cos(2 * math.pi * kk / n)), _snap(math.sin(2 * math.pi * kk / n))
        orr, oi = odd[kk]
        tr = _plus(_scaled(orr, c), _scaled(oi, s))
        ti = _plus(_scaled(oi, c), _scaled(orr, -s))
        er, ei = even[kk]
        if keep is None or kk in keep:
            out[kk] = (_plus(er, tr), _plus(ei, ti))
        if keep is None or kk + n // 2 in keep:
            out[kk + n // 2] = (_minus(er, tr), _minus(ei, ti))
    return out


def _dft16_real(xs):
    half = DFT_OUTER // 2
    return _fft_dit([(x, None) for x in xs], keep=set(range(half + 1)))[:half + 1]


def _fourier_body(u_ref, um_ref, g_ref, gt_ref, twc_ref, tws_ref, cs_ref, f_ref,
                  xs_ref, zr_ref, zi_ref, sh_ref, *, n2):
    nm = n2 - 1
    half = DFT_OUTER // 2
    xs_ref[0, 0:N_META] = um_ref[...]
    xs_ref[0, N_META:nm] = u_ref[0, 0:nm - N_META]
    for a in range(1, DFT_OUTER):
        xs_ref[a] = u_ref[0, pl.ds(n2 * a - N_META, nm), :]
    tails = [u_ref[0, pl.ds(n2 * a + nm - N_META, 1), :] for a in range(DFT_OUTER)]

    def stage1(i, carry):
        r0 = pl.multiple_of(i * SUBLANES, SUBLANES)
        zs = _dft16_real([xs_ref[a, pl.ds(r0, SUBLANES), :] for a in range(DFT_OUTER)])
        for c, (zr, zi) in enumerate(zs):
            zr_ref[c, pl.ds(r0, SUBLANES), :] = zr
            if zi is not None:
                zi_ref[c, pl.ds(r0, SUBLANES), :] = zi
        return carry

    lax.fori_loop(0, nm // SUBLANES, stage1, 0)
    zt = _dft16_real(tails)

    g = g_ref[...]
    gt = gt_ref[...]
    csm = cs_ref[...]
    inv_sqrt_n = 1.0 / math.sqrt(DFT_OUTER * n2)

    def prep(s):
        zr = zr_ref[s]
        ztr, zti = zt[s]
        zi = None if s in (0, half) else zi_ref[s]
        if s > 0:
            tc, ts = twc_ref[0:nm, s:s + 1], tws_ref[0:nm, s:s + 1]
            tct, tst = twc_ref[nm:n2, s:s + 1], tws_ref[nm:n2, s:s + 1]
            if zi is None:
                zr, zi = zr * tc, -(zr * ts)
                ztr, zti = ztr * tct, -(ztr * tst)
            else:
                zr, zi = zr * tc + zi * ts, zi * tc - zr * ts
                ztr, zti = ztr * tct + zti * tst, zti * tct - ztr * tst
        if zi is None:
            zi, zti = jnp.zeros_like(zr), jnp.zeros_like(ztr)
        z = jnp.concatenate([zr, zi], axis=1)
        tail = jnp.concatenate([ztr, zti], axis=1)
        return z, tail

    def dense(z, tail):
        p = jnp.dot(g, z.astype(BF16), preferred_element_type=F32)
        return p + gt * tail

    def channel_store(c, pr, pi):
        y = jnp.dot(jnp.concatenate([pr, pi], axis=1).astype(BF16), csm, preferred_element_type=F32)
        f_ref[0, pl.ds(c, nm, stride=DFT_OUTER), :] = y

    def post(s, p, z, tail):
        czr, czi, szr, szi = p[:nm, :LANES], p[:nm, LANES:], p[nm:, :LANES], p[nm:, LANES:]
        channel_store(s, czr + szi, czi - szr)
        if 0 < s < half:
            col = (jnp.sum(z, axis=0, keepdims=True) + tail) * inv_sqrt_n
            sh_ref[0, 0:nm] = czr - szi
            sh_ref[1, 0:nm] = -(czi + szr)
            sh_ref[0, nm:n2] = col[:, :LANES]
            sh_ref[1, nm:n2] = -col[:, LANES:]
            channel_store(DFT_OUTER - s, sh_ref[0, 1:n2], sh_ref[1, 1:n2])

    pending = None
    for s in range(half + 1):
        z, tail = prep(s)
        p = dense(z, tail)
        if pending is not None:
            post(*pending)
        pending = (s, p, z, tail)
    post(*pending)


def _fourier(u, um, g, gt, twc, tws, cs):
    B, T, _ = u.shape
    n2 = (N_META + T) // DFT_OUTER
    nm = n2 - 1
    body = functools.partial(_fourier_body, n2=n2)

    def const(arr):
        return pl.BlockSpec(arr.shape, lambda b, j: (0, 0))

    return pl.pallas_call(
        body,
        grid=(B, FN_GROUPS),
        in_specs=[
            pl.BlockSpec((1, T, LANES), lambda b, j: (b, 0, j)),
            pl.BlockSpec((N_META, LANES), lambda b, j: (0, j)),
            const(g), const(gt), const(twc), const(tws), const(cs),
        ],
        out_specs=pl.BlockSpec((1, T, LANES), lambda b, j: (b, 0, j)),
        out_shape=jax.ShapeDtypeStruct((B, T, FN_WIDTH), F32),
        scratch_shapes=[
            pltpu.VMEM((DFT_OUTER, nm, LANES), F32),
            pltpu.VMEM((DFT_OUTER // 2 + 1, nm, LANES), F32),
            pltpu.VMEM((DFT_OUTER // 2 + 1, nm, LANES), F32),
            pltpu.VMEM((2, nm + SUBLANES, LANES), F32),
        ],
        compiler_params=pltpu.CompilerParams(
            dimension_semantics=("arbitrary", "arbitrary"), vmem_limit_bytes=VMEM_LIMIT),
        name="fourier",
    )(u, um, g, gt, twc, tws, cs)


def _fourier_constants(n2):
    n = DFT_OUTER * n2
    b = jnp.arange(n2, dtype=jnp.int32)
    d = jnp.arange(1, n2, dtype=jnp.int32)
    th = ((d[:, None] * b[None, :]) % n2).astype(F32) * (2.0 * math.pi / n2)
    full = jnp.concatenate([jnp.cos(th), jnp.sin(th)], axis=0) * (1.0 / math.sqrt(n))
    g, gt = full[:, :n2 - 1].astype(BF16), full[:, n2 - 1:]
    c = jnp.arange(DFT_OUTER, dtype=jnp.int32)
    tw = ((b[:, None] * c[None, :]) % n).astype(F32) * (2.0 * math.pi / n)
    return g, gt, jnp.cos(tw), jnp.sin(tw)


def _channel_dft_matrix():
    ch = jnp.arange(FN_GROUP_DIM, dtype=jnp.int32)
    th = ((ch[:, None] * ch[None, :]) % FN_GROUP_DIM).astype(F32) * (2.0 * math.pi / FN_GROUP_DIM)
    scale = 1.0 / math.sqrt(FN_GROUP_DIM)
    return jnp.concatenate([jnp.cos(th) * scale, jnp.sin(th) * scale], axis=0).astype(BF16)


def _trunk_body(x_ref, o_ref, f_ref, gmix_ref, wg_ref, wna_ref, wfn_ref, wout_ref, gmlp_ref,
                wup_ref, wdn_ref, gfin_ref, y_ref, *, ff_chunk):
    def dot(a, b):
        return jnp.dot(a, b, preferred_element_type=F32)

    x = x_ref[0]
    h = _rms(x, gmix_ref[...]).astype(BF16)
    gates = dot(h, wg_ref[...])
    y_na = dot(o_ref[0], wna_ref[...])
    y_fn = dot(f_ref[0].astype(BF16), wfn_ref[...])
    mixed = jax.nn.sigmoid(gates[:, :D_MODEL]) * y_na + jax.nn.sigmoid(gates[:, D_MODEL:]) * y_fn
    x1 = x + dot(mixed.astype(BF16), wout_ref[...])
    h2 = _rms(x1, gmlp_ref[...]).astype(BF16)

    def up(c):
        return dot(h2, wup_ref[:, c * ff_chunk:(c + 1) * ff_chunk])

    def down(c, a):
        a = jnp.maximum(a, 0.0)
        return dot((a * a).astype(BF16), wdn_ref[c * ff_chunk:(c + 1) * ff_chunk, :])

    n_ff = D_FF // ff_chunk
    acc = x1
    pending = up(0)
    for c in range(n_ff):
        nxt = up(c + 1) if c + 1 < n_ff else None
        acc = acc + down(c, pending)
        pending = nxt
    y_ref[0] = _rms(acc, gfin_ref[...])


def _trunk(x, o, f, gmix, wg, wna, wfn, wout, gmlp, wup, wdn, gfin, tm):
    B, T, D = x.shape
    body = functools.partial(_trunk_body, ff_chunk=1024)

    def const(arr):
        return pl.BlockSpec(arr.shape, lambda b, i: (0, 0), pipeline_mode=pl.Buffered(1))

    return pl.pallas_call(
        body,
        grid=(B, T // tm),
        in_specs=[
            pl.BlockSpec((1, tm, D), lambda b, i: (b, i, 0)),
            pl.BlockSpec((1, tm, NA_WIDTH), lambda b, i: (b, i, 0)),
            pl.BlockSpec((1, tm, FN_WIDTH), lambda b, i: (b, i, 0)),
            const(gmix), const(wg), const(wna), const(wfn), const(wout), const(gmlp),
            const(wup), const(wdn), const(gfin),
        ],
        out_specs=pl.BlockSpec((1, tm, D), lambda b, i: (b, i, 0)),
        out_shape=jax.ShapeDtypeStruct((B, T, D), F32),
        compiler_params=pltpu.CompilerParams(
            dimension_semantics=("arbitrary", "arbitrary"), vmem_limit_bytes=VMEM_LIMIT),
        name="trunk",
    )(x, o, f, gmix, wg, wna, wfn, wout, gmlp, wup, wdn, gfin)


def _pad_rows(a, rows):
    return jnp.concatenate([a, jnp.zeros((rows - a.shape[0], a.shape[1]), a.dtype)], axis=0)


def _run_group(x, meta, consts):
    (w_qkvu, cs, bias, g_mix, wg, wna, wfn, wout, g_mlp, wup, wdn, g_fin) = consts
    B, T, _ = x.shape
    n2 = (N_META + T) // DFT_OUTER
    q, k, v, u = _inproj(x, g_mix, w_qkvu, tm=512)
    _, km, vm, um = meta
    o = _attention(q, k, v, _pad_rows(km, GRID_W), _pad_rows(vm, GRID_W), bias, rows_per_step=32)
    f = _fourier(u, um, *_fourier_constants(n2), cs)
    return _trunk(x, o, f, g_mix, wg, wna, wfn, wout, g_mlp, wup, wdn, g_fin, tm=512)


def kernel(x_prompt, x_sample, meta_tokens, w_in, rel_bias, meta_bias, w_branch_na, w_branch_fn,
           w_out, g_mix, g_mlp, w_up, w_down, g_final):
    assert w_in.shape[0] == 1, "single-layer trunk"
    w_in0 = w_in[0]
    split = 3 * NA_WIDTH + FN_WIDTH
    w_qkvu = w_in0[:, :split].astype(BF16)
    wg = w_in0[:, split:].astype(BF16)
    cs = _channel_dft_matrix()
    bias = _bias_tables(rel_bias[0], meta_bias[0])
    g_mix2 = g_mix[0][None]
    consts = (w_qkvu, cs, bias, g_mix2, wg, w_branch_na[0].astype(BF16), w_branch_fn[0].astype(BF16),
              w_out[0].astype(BF16), g_mlp[0][None], w_up[0].astype(BF16), w_down[0].astype(BF16),
              g_final[None])
    meta = _inproj(meta_tokens[None], g_mix2, w_qkvu, tm=N_META)
    meta = tuple(m[0] for m in meta)
    return (_run_group(x_prompt, meta, consts), _run_group(x_sample, meta, consts))
```

```python
import functools
import math

import jax
import jax.numpy as jnp
import numpy as np
from jax import lax
from jax.experimental import pallas as pl
from jax.experimental.pallas import tpu as pltpu

F32 = jnp.float32
BF16 = jnp.bfloat16

D_MODEL = 1024
N_META = 16
GRID_W = 64
NA_HEADS = 8
NA_HEAD_DIM = 64
NA_WIDTH = NA_HEADS * NA_HEAD_DIM
NA_WIN_ROWS = 8
NA_WIN_COLS = 16
FN_GROUPS = 4
FN_GROUP_DIM = 128
FN_WIDTH = FN_GROUPS * FN_GROUP_DIM
D_FF = 4 * D_MODEL
RMS_EPS = 1e-6

LANES = 128
SUBLANES = 8
HEAD_PAIRS = NA_HEADS // 2
WIN_SLOTS = NA_WIN_ROWS + 2
WIN_KEYS = WIN_SLOTS * GRID_W
LOCAL_KEYS = (NA_WIN_ROWS + 1) * GRID_W
SCORES_AHEAD = 2
NEG = -1e30
LOG2E = math.log2(math.e)
DFT_OUTER = 16
VMEM_LIMIT = 56 * 1024 * 1024


def _rms(x, g):
    ms = jnp.mean(x * x, axis=-1, keepdims=True)
    return x * lax.rsqrt(ms + RMS_EPS) * g


def _inproj_body(x_ref, g_ref, w_ref, q_ref, k_ref, v_ref, u_ref):
    h = _rms(x_ref[0], g_ref[...]).astype(BF16)
    z = jnp.dot(h, w_ref[...], preferred_element_type=F32)
    q_ref[0] = (z[:, :NA_WIDTH] * (NA_HEAD_DIM ** -0.5 * LOG2E)).astype(BF16)
    k_ref[0] = z[:, NA_WIDTH:2 * NA_WIDTH].astype(BF16)
    v_ref[0] = z[:, 2 * NA_WIDTH:3 * NA_WIDTH].astype(BF16)
    u_ref[0] = z[:, 3 * NA_WIDTH:]


def _inproj(x, g, w, tm):
    B, T, D = x.shape
    tok = pl.BlockSpec((1, tm, NA_WIDTH), lambda b, i: (b, i, 0))
    return pl.pallas_call(
        _inproj_body,
        grid=(B, T // tm),
        in_specs=[
            pl.BlockSpec((1, tm, D), lambda b, i: (b, i, 0)),
            pl.BlockSpec((1, D), lambda b, i: (0, 0)),
            pl.BlockSpec(w.shape, lambda b, i: (0, 0)),
        ],
        out_specs=[tok] * 4,
        out_shape=[jax.ShapeDtypeStruct((B, T, NA_WIDTH), BF16)] * 3
        + [jax.ShapeDtypeStruct((B, T, FN_WIDTH), F32)],
        compiler_params=pltpu.CompilerParams(
            dimension_semantics=("arbitrary", "arbitrary"), vmem_limit_bytes=VMEM_LIMIT),
        name="inproj",
    )(x, g, w)


def _attn_body(q_ref, k_ref, v_ref, km_ref, vm_ref, bias_ref, o_ref, *, rows, rows_per_step):
    chunk = pl.program_id(2)
    first_head = lax.broadcasted_iota(jnp.int32, (2 * GRID_W, LANES), 1) < NA_HEAD_DIM
    first_head_kv = lax.broadcasted_iota(jnp.int32, (WIN_KEYS, LANES), 1) < NA_HEAD_DIM
    km = km_ref[...]
    vm = vm_ref[...]
    zero = jnp.zeros((2 * GRID_W, LANES), BF16)
    ones = jnp.ones((WIN_KEYS, LANES), BF16)

    def scores(i):
        r = chunk * rows_per_step + 2 * i
        start = jnp.clip(r - NA_WIN_ROWS // 2, 0, rows - (NA_WIN_ROWS + 1))
        cfg = jnp.where(r == 0, 0, jnp.where(r == 2, 1, jnp.where(
            r == rows - 4, 3, jnp.where(r == rows - 2, 4, 2))))
        q2 = q_ref[0, i * 2 * GRID_W:(i + 1) * 2 * GRID_W, :]
        qq = jnp.concatenate([jnp.where(first_head, q2, zero), jnp.where(first_head, zero, q2)], axis=0)
        koff = pl.multiple_of(start * GRID_W, GRID_W)
        kw = jnp.concatenate([k_ref[0, pl.ds(koff, LOCAL_KEYS), :], km], axis=0)
        s = lax.dot_general(qq, kw, (((1,), (1,)), ((), ())), preferred_element_type=F32)
        bias = jnp.concatenate([bias_ref[0, 0, cfg], bias_ref[0, 1, cfg]], axis=0)
        return s + bias, koff

    def finish(i, s, koff):
        vw = jnp.concatenate([v_ref[0, pl.ds(koff, LOCAL_KEYS), :], vm], axis=0)
        v0 = jnp.where(first_head_kv, vw, ones)
        v1 = jnp.where(first_head_kv, ones, vw)
        m = jnp.max(s, axis=-1, keepdims=True)
        p = jnp.exp2(s - m).astype(BF16)
        o0 = jnp.dot(p[:2 * GRID_W], v0, preferred_element_type=F32)
        o1 = jnp.dot(p[2 * GRID_W:], v1, preferred_element_type=F32)
        r0 = o0 / pltpu.roll(o0, NA_HEAD_DIM, axis=1)
        r1 = o1 / pltpu.roll(o1, NA_HEAD_DIM, axis=1)
        o_ref[0, i * 2 * GRID_W:(i + 1) * 2 * GRID_W, :] = jnp.where(first_head, r0, r1).astype(BF16)

    n_pairs = rows_per_step // 2
    queue = [scores(i) for i in range(min(SCORES_AHEAD, n_pairs))]
    for i in range(n_pairs):
        if i + SCORES_AHEAD < n_pairs:
            queue.append(scores(i + SCORES_AHEAD))
        finish(i, *queue.pop(0))


def _attention(q, k, v, km, vm, bias, rows_per_step):
    B, T, _ = q.shape
    rows = T // GRID_W
    tq = rows_per_step * GRID_W
    body = functools.partial(_attn_body, rows=rows, rows_per_step=rows_per_step)
    return pl.pallas_call(
        body,
        grid=(B, HEAD_PAIRS, rows // rows_per_step),
        in_specs=[
            pl.BlockSpec((1, tq, LANES), lambda b, h, c: (b, c, h)),
            pl.BlockSpec((1, T, LANES), lambda b, h, c: (b, 0, h)),
            pl.BlockSpec((1, T, LANES), lambda b, h, c: (b, 0, h)),
            pl.BlockSpec((GRID_W, LANES), lambda b, h, c: (0, h)),
            pl.BlockSpec((GRID_W, LANES), lambda b, h, c: (0, h)),
            pl.BlockSpec((1,) + bias.shape[1:], lambda b, h, c: (h, 0, 0, 0, 0)),
        ],
        out_specs=pl.BlockSpec((1, tq, LANES), lambda b, h, c: (b, c, h)),
        out_shape=jax.ShapeDtypeStruct((B, T, NA_WIDTH), BF16),
        compiler_params=pltpu.CompilerParams(
            dimension_semantics=("arbitrary", "arbitrary", "arbitrary"), vmem_limit_bytes=VMEM_LIMIT),
        name="attention",
    )(q, k, v, km, vm, bias)


def _bias_tables(rel_bias, meta_bias):
    c = np.arange(GRID_W)[:, None]
    j = np.arange(GRID_W)[None, :]
    cs = np.clip(c - NA_WIN_COLS // 2, 0, GRID_W - NA_WIN_COLS)
    valid = (j >= cs) & (j < cs + NA_WIN_COLS)
    n_off = 2 * NA_WIN_COLS - 1
    onehot = (valid[None] & ((j - c + NA_WIN_COLS - 1)[None] == np.arange(n_off)[:, None, None])).astype(np.float32)
    toe = jnp.einsum("hrx,xcj->hrcj", rel_bias * LOG2E, onehot, precision=lax.Precision.HIGHEST)
    toe = toe + np.where(valid, 0.0, NEG).astype(np.float32)
    negblk = jnp.full((NA_HEADS, GRID_W, GRID_W), NEG, F32)
    metablk = jnp.concatenate(
        [jnp.broadcast_to(meta_bias[:, None, :] * LOG2E, (NA_HEADS, GRID_W, N_META)),
         jnp.full((NA_HEADS, GRID_W, GRID_W - N_META), NEG, F32)], axis=-1)
    cfgs = [((0, 1), ((0, 8), (0, 8))), ((2, 3), ((0, 8), (0, 8))), ((4, 5), ((0, 8), (1, 9))),
            ((5, 6), ((1, 9), (1, 9))), ((7, 8), ((1, 9), (1, 9)))]
    per_cfg = []
    for qslots, wins in cfgs:
        per_row = []
        for a, (w0, w1) in zip(qslots, wins):
            blocks = [toe[:, w - a + NA_WIN_ROWS - 1] if w0 <= w < w1 else negblk
                      for w in range(NA_WIN_ROWS + 1)]
            per_row.append(jnp.concatenate(blocks + [metablk], axis=-1))
        per_cfg.append(jnp.concatenate(per_row, axis=1))
    t = jnp.stack(per_cfg, axis=1)
    return t.reshape(HEAD_PAIRS, 2, len(cfgs), 2 * GRID_W, WIN_KEYS)


def _snap(x):
    for v in (0.0, 1.0, -1.0):
        if abs(x - v) < 1e-12:
            return v
    return x


def _scaled(x, coef):
    if x is None or coef == 0.0:
        return None
    if coef == 1.0:
        return x
    if coef == -1.0:
        return -x
    return x * coef


def _plus(a, b):
    if a is None:
        return b
    if b is None:
        return a
    return a + b


def _minus(a, b):
    if b is None:
        return a
    if a is None:
        return -b
    return a - b


def _fft_dit(zs, keep=None):
    n = len(zs)
    if n == 1:
        return zs
    even, odd = _fft_dit(zs[0::2]), _fft_dit(zs[1::2])
    out = [None] * n
    for kk in range(n // 2):
        c, s = _snap(math.cos(2 * math.pi * kk / n)), _snap(math.sin(2 * math.pi * kk / n))
        orr, oi = odd[kk]
        tr = _plus(_scaled(orr, c), _scaled(oi, s))
        ti = _plus(_scaled(oi, c), _scaled(orr, -s))
        er, ei = even[kk]
        if keep is None or kk in keep:
            out[kk] = (_plus(er, tr), _plus(ei, ti))
        if keep is None or kk + n // 2 in keep:
            out[kk + n // 2] = (_minus(er, tr), _minus(ei, ti))
    return out


def _dft16_real(xs):
    half = DFT_OUTER // 2
    return _fft_dit([(x, None) for x in xs], keep=set(range(half + 1)))[:half + 1]


def _fourier_body(u_ref, um_ref, g_ref, gt_ref, twc_ref, tws_ref, cs_ref, f_ref,
                  xs_ref, zr_ref, zi_ref, sh_ref, *, n2):
    nm = n2 - 1
    half = DFT_OUTER // 2
    xs_ref[0, 0:N_META] = um_ref[...]
    xs_ref[0, N_META:nm] = u_ref[0, 0:nm - N_META]
    for a in range(1, DFT_OUTER):
        xs_ref[a] = u_ref[0, pl.ds(n2 * a - N_META, nm), :]
    tails = [u_ref[0, pl.ds(n2 * a + nm - N_META, 1), :] for a in range(DFT_OUTER)]

    def stage1(i, carry):
        r0 = pl.multiple_of(i * SUBLANES, SUBLANES)
        zs = _dft16_real([xs_ref[a, pl.ds(r0, SUBLANES), :] for a in range(DFT_OUTER)])
        for c, (zr, zi) in enumerate(zs):
            zr_ref[c, pl.ds(r0, SUBLANES), :] = zr
            if zi is not None:
                zi_ref[c, pl.ds(r0, SUBLANES), :] = zi
        return carry

    lax.fori_loop(0, nm // SUBLANES, stage1, 0)
    zt = _dft16_real(tails)

    g = g_ref[...]
    gt = gt_ref[...]
    csm = cs_ref[...]
    inv_sqrt_n = 1.0 / math.sqrt(DFT_OUTER * n2)

    def prep(s):
        zr = zr_ref[s]
        ztr, zti = zt[s]
        zi = None if s in (0, half) else zi_ref[s]
        if s > 0:
            tc, ts = twc_ref[0:nm, s:s + 1], tws_ref[0:nm, s:s + 1]
            tct, tst = twc_ref[nm:n2, s:s + 1], tws_ref[nm:n2, s:s + 1]
            if zi is None:
                zr, zi = zr * tc, -(zr * ts)
                ztr, zti = ztr * tct, -(ztr * tst)
            else:
                zr, zi = zr * tc + zi * ts, zi * tc - zr * ts
                ztr, zti = ztr * tct + zti * tst, zti * tct - ztr * tst
        if zi is None:
            zi, zti = jnp.zeros_like(zr), jnp.zeros_like(ztr)
        z = jnp.concatenate([zr, zi], axis=1)
        tail = jnp.concatenate([ztr, zti], axis=1)
        return z, tail

    def dense(z, tail):
        p = jnp.dot(g, z.astype(BF16), preferred_element_type=F32)
        return p + gt * tail

    def channel_store(c, pr, pi):
        y = jnp.dot(jnp.concatenate([pr, pi], axis=1).astype(BF16), csm, preferred_element_type=F32)
        f_ref[0, pl.ds(c, nm, stride=DFT_OUTER), :] = y

    def post(s, p, z, tail):
        czr, czi, szr, szi = p[:nm, :LANES], p[:nm, LANES:], p[nm:, :LANES], p[nm:, LANES:]
        channel_store(s, czr + szi, czi - szr)
        if 0 < s < half:
            col = (jnp.sum(z, axis=0, keepdims=True) + tail) * inv_sqrt_n
            sh_ref[0, 0:nm] = czr - szi
            sh_ref[1, 0:nm] = -(czi + szr)
            sh_ref[0, nm:n2] = col[:, :LANES]
            sh_ref[1, nm:n2] = -col[:, LANES:]
            channel_store(DFT_OUTER - s, sh_ref[0, 1:n2], sh_ref[1, 1:n2])

    pending = None
    for s in range(half + 1):
        z, tail = prep(s)
        p = dense(z, tail)
        if pending is not None:
            post(*pending)
        pending = (s, p, z, tail)
    post(*pending)


def _fourier(u, um, g, gt, twc, tws, cs):
    B, T, _ = u.shape
    n2 = (N_META + T) // DFT_OUTER
    nm = n2 - 1
    body = functools.partial(_fourier_body, n2=n2)

    def const(arr):
        return pl.BlockSpec(arr.shape, lambda b, j: (0, 0))

    return pl.pallas_call(
        body,
        grid=(B, FN_GROUPS),
        in_specs=[
            pl.BlockSpec((1, T, LANES), lambda b, j: (b, 0, j)),
            pl.BlockSpec((N_META, LANES), lambda b, j: (0, j)),
            const(g), const(gt), const(twc), const(tws), const(cs),
        ],
        out_specs=pl.BlockSpec((1, T, LANES), lambda b, j: (b, 0, j)),
        out_shape=jax.ShapeDtypeStruct((B, T, FN_WIDTH), F32),
        scratch_shapes=[
            pltpu.VMEM((DFT_OUTER, nm, LANES), F32),
            pltpu.VMEM((DFT_OUTER // 2 + 1, nm, LANES), F32),
            pltpu.VMEM((DFT_OUTER // 2 + 1, nm, LANES), F32),
            pltpu.VMEM((2, nm + SUBLANES, LANES), F32),
        ],
        compiler_params=pltpu.CompilerParams(
            dimension_semantics=("arbitrary", "arbitrary"), vmem_limit_bytes=VMEM_LIMIT),
        name="fourier",
    )(u, um, g, gt, twc, tws, cs)


def _fourier_constants(n2):
    n = DFT_OUTER * n2
    b = jnp.arange(n2, dtype=jnp.int32)
    d = jnp.arange(1, n2, dtype=jnp.int32)
    th = ((d[:, None] * b[None, :]) % n2).astype(F32) * (2.0 * math.pi / n2)
    full = jnp.concatenate([jnp.cos(th), jnp.sin(th)], axis=0) * (1.0 / math.sqrt(n))
    g, gt = full[:, :n2 - 1].astype(BF16), full[:, n2 - 1:]
    c = jnp.arange(DFT_OUTER, dtype=jnp.int32)
    tw = ((b[:, None] * c[None, :]) % n).astype(F32) * (2.0 * math.pi / n)
    return g, gt, jnp.cos(tw), jnp.sin(tw)


def _channel_dft_matrix():
    ch = jnp.arange(FN_GROUP_DIM, dtype=jnp.int32)
    th = ((ch[:, None] * ch[None, :]) % FN_GROUP_DIM).astype(F32) * (2.0 * math.pi / FN_GROUP_DIM)
    scale = 1.0 / math.sqrt(FN_GROUP_DIM)
    return jnp.concatenate([jnp.cos(th) * scale, jnp.sin(th) * scale], axis=0).astype(BF16)


def _trunk_body(x_ref, o_ref, f_ref, gmix_ref, wg_ref, wna_ref, wfn_ref, wout_ref, gmlp_ref,
                wup_ref, wdn_ref, gfin_ref, y_ref, *, ff_chunk):
    def dot(a, b):
        return jnp.dot(a, b, preferred_element_type=F32)

    x = x_ref[0]
    h = _rms(x, gmix_ref[...]).astype(BF16)
    gates = dot(h, wg_ref[...])
    y_na = dot(o_ref[0], wna_ref[...])
    y_fn = dot(f_ref[0].astype(BF16), wfn_ref[...])
    mixed = jax.nn.sigmoid(gates[:, :D_MODEL]) * y_na + jax.nn.sigmoid(gates[:, D_MODEL:]) * y_fn
    x1 = x + dot(mixed.astype(BF16), wout_ref[...])
    h2 = _rms(x1, gmlp_ref[...]).astype(BF16)

    def up(c):
        return dot(h2, wup_ref[:, c * ff_chunk:(c + 1) * ff_chunk])

    def down(c, a):
        a = jnp.maximum(a, 0.0)
        return dot((a * a).astype(BF16), wdn_ref[c * ff_chunk:(c + 1) * ff_chunk, :])

    n_ff = D_FF // ff_chunk
    acc = x1
    pending = up(0)
    for c in range(n_ff):
        nxt = up(c + 1) if c + 1 < n_ff else None
        acc = acc + down(c, pending)
        pending = nxt
    y_ref[0] = _rms(acc, gfin_ref[...])


def _trunk(x, o, f, gmix, wg, wna, wfn, wout, gmlp, wup, wdn, gfin, tm):
    B, T, D = x.shape
    body = functools.partial(_trunk_body, ff_chunk=1024)

    def const(arr):
        return pl.BlockSpec(arr.shape, lambda b, i: (0, 0), pipeline_mode=pl.Buffered(1))

    return pl.pallas_call(
        body,
        grid=(B, T // tm),
        in_specs=[
            pl.BlockSpec((1, tm, D), lambda b, i: (b, i, 0)),
            pl.BlockSpec((1, tm, NA_WIDTH), lambda b, i: (b, i, 0)),
            pl.BlockSpec((1, tm, FN_WIDTH), lambda b, i: (b, i, 0)),
            const(gmix), const(wg), const(wna), const(wfn), const(wout), const(gmlp),
            const(wup), const(wdn), const(gfin),
        ],
        out_specs=pl.BlockSpec((1, tm, D), lambda b, i: (b, i, 0)),
        out_shape=jax.ShapeDtypeStruct((B, T, D), F32),
        compiler_params=pltpu.CompilerParams(
            dimension_semantics=("arbitrary", "arbitrary"), vmem_limit_bytes=VMEM_LIMIT),
        name="trunk",
    )(x, o, f, gmix, wg, wna, wfn, wout, gmlp, wup, wdn, gfin)


def _pad_rows(a, rows):
    return jnp.concatenate([a, jnp.zeros((rows - a.shape[0], a.shape[1]), a.dtype)], axis=0)


def _run_group(x, meta, consts):
    (w_qkvu, cs, bias, g_mix, wg, wna, wfn, wout, g_mlp, wup, wdn, g_fin) = consts
    B, T, _ = x.shape
    n2 = (N_META + T) // DFT_OUTER
    q, k, v, u = _inproj(x, g_mix, w_qkvu, tm=512)
    _, km, vm, um = meta
    o = _attention(q, k, v, _pad_rows(km, GRID_W), _pad_rows(vm, GRID_W), bias, rows_per_step=32)
    f = _fourier(u, um, *_fourier_constants(n2), cs)
    return _trunk(x, o, f, g_mix, wg, wna, wfn, wout, g_mlp, wup, wdn, g_fin, tm=512)


def kernel(x_prompt, x_sample, meta_tokens, w_in, rel_bias, meta_bias, w_branch_na, w_branch_fn,
           w_out, g_mix, g_mlp, w_up, w_down, g_final):
    assert w_in.shape[0] == 1, "single-layer trunk"
    w_in0 = w_in[0]
    split = 3 * NA_WIDTH + FN_WIDTH
    w_qkvu = w_in0[:, :split].astype(BF16)
    wg = w_in0[:, split:].astype(BF16)
    cs = _channel_dft_matrix()
    bias = _bias_tables(rel_bias[0], meta_bias[0])
    g_mix2 = g_mix[0][None]
    consts = (w_qkvu, cs, bias, g_mix2, wg, w_branch_na[0].astype(BF16), w_branch_fn[0].astype(BF16),
              w_out[0].astype(BF16), g_mlp[0][None], w_up[0].astype(BF16), w_down[0].astype(BF16),
              g_final[None])
    meta = _inproj(meta_tokens[None], g_mix2, w_qkvu, tm=N_META)
    meta = tuple(m[0] for m in meta)
    return (_run_group(x_prompt, meta, consts), _run_group(x_sample, meta, consts))
```

```python
import functools
import math

import jax
import jax.numpy as jnp
import numpy as np
from jax import lax
from jax.experimental import pallas as pl
from jax.experimental.pallas import tpu as pltpu

F32 = jnp.float32
BF16 = jnp.bfloat16

D_MODEL = 1024
N_META = 16
GRID_W = 64
NA_HEADS = 8
NA_HEAD_DIM = 64
NA_WIDTH = NA_HEADS * NA_HEAD_DIM
NA_WIN_ROWS = 8
NA_WIN_COLS = 16
FN_GROUPS = 4
FN_GROUP_DIM = 128
FN_WIDTH = FN_GROUPS * FN_GROUP_DIM
D_FF = 4 * D_MODEL
RMS_EPS = 1e-6

LANES = 128
SUBLANES = 8
HEAD_PAIRS = NA_HEADS // 2
WIN_SLOTS = NA_WIN_ROWS + 2
WIN_KEYS = WIN_SLOTS * GRID_W
LOCAL_KEYS = (NA_WIN_ROWS + 1) * GRID_W
SCORES_AHEAD = 2
ROW_BLOCK = 256
NEG = -1e30
LOG2E = math.log2(math.e)
DFT_OUTER = 16
VMEM_LIMIT = 56 * 1024 * 1024


def _rms(x, g):
    ms = jnp.mean(x * x, axis=-1, keepdims=True)
    return x * lax.rsqrt(ms + RMS_EPS) * g


def _inproj_body(x_ref, g_ref, w_ref, q_ref, k_ref, v_ref, u_ref, *, n_sub):
    sub = x_ref.shape[1] // n_sub
    for r in range(n_sub):
        rows = slice(r * sub, (r + 1) * sub)
        h = _rms(x_ref[0, rows, :], g_ref[...]).astype(BF16)
        z = jnp.dot(h, w_ref[...], preferred_element_type=F32)
        q_ref[0, rows, :] = (z[:, :NA_WIDTH] * (NA_HEAD_DIM ** -0.5 * LOG2E)).astype(BF16)
        k_ref[0, rows, :] = z[:, NA_WIDTH:2 * NA_WIDTH].astype(BF16)
        v_ref[0, rows, :] = z[:, 2 * NA_WIDTH:3 * NA_WIDTH].astype(BF16)
        u_ref[0, rows, :] = z[:, 3 * NA_WIDTH:]


def _inproj(x, g, w, tm):
    B, T, D = x.shape
    tok = pl.BlockSpec((1, tm, NA_WIDTH), lambda b, i: (b, i, 0))
    return pl.pallas_call(
        functools.partial(_inproj_body, n_sub=max(tm // ROW_BLOCK, 1)),
        grid=(B, T // tm),
        in_specs=[
            pl.BlockSpec((1, tm, D), lambda b, i: (b, i, 0)),
            pl.BlockSpec((1, D), lambda b, i: (0, 0)),
            pl.BlockSpec(w.shape, lambda b, i: (0, 0)),
        ],
        out_specs=[tok] * 4,
        out_shape=[jax.ShapeDtypeStruct((B, T, NA_WIDTH), BF16)] * 3
        + [jax.ShapeDtypeStruct((B, T, FN_WIDTH), F32)],
        compiler_params=pltpu.CompilerParams(
            dimension_semantics=("arbitrary", "arbitrary"), vmem_limit_bytes=VMEM_LIMIT),
        name="inproj",
    )(x, g, w)


def _attn_body(q_ref, k_ref, v_ref, km_ref, vm_ref, bias_ref, o_ref, *, rows, rows_per_step):
    chunk = pl.program_id(2)
    first_head = lax.broadcasted_iota(jnp.int32, (2 * GRID_W, LANES), 1) < NA_HEAD_DIM
    first_head_kv = lax.broadcasted_iota(jnp.int32, (WIN_KEYS, LANES), 1) < NA_HEAD_DIM
    km = km_ref[...]
    vm = vm_ref[...]
    zero = jnp.zeros((2 * GRID_W, LANES), BF16)
    ones = jnp.ones((WIN_KEYS, LANES), BF16)

    def scores(i):
        r = chunk * rows_per_step + 2 * i
        start = jnp.clip(r - NA_WIN_ROWS // 2, 0, rows - (NA_WIN_ROWS + 1))
        cfg = jnp.where(r == 0, 0, jnp.where(r == 2, 1, jnp.where(
            r == rows - 4, 3, jnp.where(r == rows - 2, 4, 2))))
        q2 = q_ref[0, i * 2 * GRID_W:(i + 1) * 2 * GRID_W, :]
        qq = jnp.concatenate([jnp.where(first_head, q2, zero), jnp.where(first_head, zero, q2)], axis=0)
        koff = pl.multiple_of(start * GRID_W, GRID_W)
        kw = jnp.concatenate([k_ref[0, pl.ds(koff, LOCAL_KEYS), :], km], axis=0)
        s = lax.dot_general(qq, kw, (((1,), (1,)), ((), ())), preferred_element_type=F32)
        bias = jnp.concatenate([bias_ref[0, 0, cfg], bias_ref[0, 1, cfg]], axis=0)
        return s + bias, koff

    def finish(i, s, koff):
        vw = jnp.concatenate([v_ref[0, pl.ds(koff, LOCAL_KEYS), :], vm], axis=0)
        v0 = jnp.where(first_head_kv, vw, ones)
        v1 = jnp.where(first_head_kv, ones, vw)
        m = jnp.max(s, axis=-1, keepdims=True)
        p = jnp.exp2(s - m).astype(BF16)
        o0 = jnp.dot(p[:2 * GRID_W], v0, preferred_element_type=F32)
        o1 = jnp.dot(p[2 * GRID_W:], v1, preferred_element_type=F32)
        r0 = o0 / pltpu.roll(o0, NA_HEAD_DIM, axis=1)
        r1 = o1 / pltpu.roll(o1, NA_HEAD_DIM, axis=1)
        o_ref[0, i * 2 * GRID_W:(i + 1) * 2 * GRID_W, :] = jnp.where(first_head, r0, r1).astype(BF16)

    n_pairs = rows_per_step // 2
    queue = [scores(i) for i in range(min(SCORES_AHEAD, n_pairs))]
    for i in range(n_pairs):
        if i + SCORES_AHEAD < n_pairs:
            queue.append(scores(i + SCORES_AHEAD))
        finish(i, *queue.pop(0))


def _attention(q, k, v, km, vm, bias, rows_per_step):
    B, T, _ = q.shape
    rows = T // GRID_W
    tq = rows_per_step * GRID_W
    body = functools.partial(_attn_body, rows=rows, rows_per_step=rows_per_step)
    return pl.pallas_call(
        body,
        grid=(B, HEAD_PAIRS, rows // rows_per_step),
        in_specs=[
            pl.BlockSpec((1, tq, LANES), lambda b, h, c: (b, c, h)),
            pl.BlockSpec((1, T, LANES), lambda b, h, c: (b, 0, h)),
            pl.BlockSpec((1, T, LANES), lambda b, h, c: (b, 0, h)),
            pl.BlockSpec((GRID_W, LANES), lambda b, h, c: (0, h)),
            pl.BlockSpec((GRID_W, LANES), lambda b, h, c: (0, h)),
            pl.BlockSpec((1,) + bias.shape[1:], lambda b, h, c: (h, 0, 0, 0, 0)),
        ],
        out_specs=pl.BlockSpec((1, tq, LANES), lambda b, h, c: (b, c, h)),
        out_shape=jax.ShapeDtypeStruct((B, T, NA_WIDTH), BF16),
        compiler_params=pltpu.CompilerParams(
            dimension_semantics=("arbitrary", "arbitrary", "arbitrary"), vmem_limit_bytes=VMEM_LIMIT),
        name="attention",
    )(q, k, v, km, vm, bias)


def _bias_tables(rel_bias, meta_bias):
    c = np.arange(GRID_W)[:, None]
    j = np.arange(GRID_W)[None, :]
    cs = np.clip(c - NA_WIN_COLS // 2, 0, GRID_W - NA_WIN_COLS)
    valid = (j >= cs) & (j < cs + NA_WIN_COLS)
    n_off = 2 * NA_WIN_COLS - 1
    onehot = (valid[None] & ((j - c + NA_WIN_COLS - 1)[None] == np.arange(n_off)[:, None, None])).astype(np.float32)
    toe = jnp.einsum("hrx,xcj->hrcj", rel_bias * LOG2E, onehot, precision=lax.Precision.HIGHEST)
    toe = toe + np.where(valid, 0.0, NEG).astype(np.float32)
    negblk = jnp.full((NA_HEADS, GRID_W, GRID_W), NEG, F32)
    metablk = jnp.concatenate(
        [jnp.broadcast_to(meta_bias[:, None, :] * LOG2E, (NA_HEADS, GRID_W, N_META)),
         jnp.full((NA_HEADS, GRID_W, GRID_W - N_META), NEG, F32)], axis=-1)
    cfgs = [((0, 1), ((0, 8), (0, 8))), ((2, 3), ((0, 8), (0, 8))), ((4, 5), ((0, 8), (1, 9))),
            ((5, 6), ((1, 9), (1, 9))), ((7, 8), ((1, 9), (1, 9)))]
    per_cfg = []
    for qslots, wins in cfgs:
        per_row = []
        for a, (w0, w1) in zip(qslots, wins):
            blocks = [toe[:, w - a + NA_WIN_ROWS - 1] if w0 <= w < w1 else negblk
                      for w in range(NA_WIN_ROWS + 1)]
            per_row.append(jnp.concatenate(blocks + [metablk], axis=-1))
        per_cfg.append(jnp.concatenate(per_row, axis=1))
    t = jnp.stack(per_cfg, axis=1)
    return t.reshape(HEAD_PAIRS, 2, len(cfgs), 2 * GRID_W, WIN_KEYS)


def _snap(x):
    for v in (0.0, 1.0, -1.0):
        if abs(x - v) < 1e-12:
            return v
    return x


def _scaled(x, coef):
    if x is None or coef == 0.0:
        return None
    if coef == 1.0:
        return x
    if coef == -1.0:
        return -x
    return x * coef


def _plus(a, b):
    if a is None:
        return b
    if b is None:
        return a
    return a + b


def _minus(a, b):
    if b is None:
        return a
    if a is None:
        return -b
    return a - b


def _fft_dit(zs, keep=None):
    n = len(zs)
    if n == 1:
        return zs
    even, odd = _fft_dit(zs[0::2]), _fft_dit(zs[1::2])
    out = [None] * n
    for kk in range(n // 2):
        c, s = _snap(math.cos(2 * math.pi * kk / n)), _snap(math.sin(2 * math.pi * kk / n))
        orr, oi = odd[kk]
        tr = _plus(_scaled(orr, c), _scaled(oi, s))
        ti = _plus(_scaled(oi, c), _scaled(orr, -s))
        er, ei = even[kk]
        if keep is None or kk in keep:
            out[kk] = (_plus(er, tr), _plus(ei, ti))
        if keep is None or kk + n // 2 in keep:
            out[kk + n // 2] = (_minus(er, tr), _minus(ei, ti))
    return out


def _dft16_real(xs):
    half = DFT_OUTER // 2
    return _fft_dit([(x, None) for x in xs], keep=set(range(half + 1)))[:half + 1]


def _fourier_body(u_ref, um_ref, g_ref, gt_ref, twc_ref, tws_ref, cs_ref, f_ref,
                  xs_ref, zr_ref, zi_ref, sh_ref, *, n2):
    nm = n2 - 1
    half = DFT_OUTER // 2
    xs_ref[0, 0:N_META] = um_ref[...]
    xs_ref[0, N_META:nm] = u_ref[0, 0:nm - N_META]
    for a in range(1, DFT_OUTER):
        xs_ref[a] = u_ref[0, pl.ds(n2 * a - N_META, nm), :]
    tails = [u_ref[0, pl.ds(n2 * a + nm - N_META, 1), :] for a in range(DFT_OUTER)]

    def stage1(i, carry):
        r0 = pl.multiple_of(i * SUBLANES, SUBLANES)
        zs = _dft16_real([xs_ref[a, pl.ds(r0, SUBLANES), :] for a in range(DFT_OUTER)])
        for c, (zr, zi) in enumerate(zs):
            zr_ref[c, pl.ds(r0, SUBLANES), :] = zr
            if zi is not None:
                zi_ref[c, pl.ds(r0, SUBLANES), :] = zi
        return carry

    lax.fori_loop(0, nm // SUBLANES, stage1, 0)
    zt = _dft16_real(tails)

    g = g_ref[...]
    gt = gt_ref[...]
    csm = cs_ref[...]
    inv_sqrt_n = 1.0 / math.sqrt(DFT_OUTER * n2)

    def prep(s):
        zr = zr_ref[s]
        ztr, zti = zt[s]
        zi = None if s in (0, half) else zi_ref[s]
        if s > 0:
            tc, ts = twc_ref[0:nm, s:s + 1], tws_ref[0:nm, s:s + 1]
            tct, tst = twc_ref[nm:n2, s:s + 1], tws_ref[nm:n2, s:s + 1]
            if zi is None:
                zr, zi = zr * tc, -(zr * ts)
                ztr, zti = ztr * tct, -(ztr * tst)
            else:
                zr, zi = zr * tc + zi * ts, zi * tc - zr * ts
                ztr, zti = ztr * tct + zti * tst, zti * tct - ztr * tst
        if zi is None:
            zi, zti = jnp.zeros_like(zr), jnp.zeros_like(ztr)
        z = jnp.concatenate([zr, zi], axis=1)
        tail = jnp.concatenate([ztr, zti], axis=1)
        return z, tail

    def dense(z, tail):
        p = jnp.dot(g, z.astype(BF16), preferred_element_type=F32)
        return p + gt * tail

    def channel_store(c, pr, pi):
        y = jnp.dot(jnp.concatenate([pr, pi], axis=1).astype(BF16), csm, preferred_element_type=F32)
        f_ref[0, pl.ds(c, nm, stride=DFT_OUTER), :] = y

    def post(s, p, z, tail):
        czr, czi, szr, szi = p[:nm, :LANES], p[:nm, LANES:], p[nm:, :LANES], p[nm:, LANES:]
        channel_store(s, czr + szi, czi - szr)
        if 0 < s < half:
            col = (jnp.sum(z, axis=0, keepdims=True) + tail) * inv_sqrt_n
            sh_ref[0, 0:nm] = czr - szi
            sh_ref[1, 0:nm] = -(czi + szr)
            sh_ref[0, nm:n2] = col[:, :LANES]
            sh_ref[1, nm:n2] = -col[:, LANES:]
            channel_store(DFT_OUTER - s, sh_ref[0, 1:n2], sh_ref[1, 1:n2])

    pending = None
    for s in range(half + 1):
        z, tail = prep(s)
        p = dense(z, tail)
        if pending is not None:
            post(*pending)
        pending = (s, p, z, tail)
    post(*pending)


def _fourier(u, um, g, gt, twc, tws, cs):
    B, T, _ = u.shape
    n2 = (N_META + T) // DFT_OUTER
    nm = n2 - 1
    body = functools.partial(_fourier_body, n2=n2)

    def const(arr):
        return pl.BlockSpec(arr.shape, lambda b, j: (0, 0))

    return pl.pallas_call(
        body,
        grid=(B, FN_GROUPS),
        in_specs=[
            pl.BlockSpec((1, T, LANES), lambda b, j: (b, 0, j)),
            pl.BlockSpec((N_META, LANES), lambda b, j: (0, j)),
            const(g), const(gt), const(twc), const(tws), const(cs),
        ],
        out_specs=pl.BlockSpec((1, T, LANES), lambda b, j: (b, 0, j)),
        out_shape=jax.ShapeDtypeStruct((B, T, FN_WIDTH), F32),
        scratch_shapes=[
            pltpu.VMEM((DFT_OUTER, nm, LANES), F32),
            pltpu.VMEM((DFT_OUTER // 2 + 1, nm, LANES), F32),
            pltpu.VMEM((DFT_OUTER // 2 + 1, nm, LANES), F32),
            pltpu.VMEM((2, nm + SUBLANES, LANES), F32),
        ],
        compiler_params=pltpu.CompilerParams(
            dimension_semantics=("arbitrary", "arbitrary"), vmem_limit_bytes=VMEM_LIMIT),
        name="fourier",
    )(u, um, g, gt, twc, tws, cs)


def _fourier_constants(n2):
    n = DFT_OUTER * n2
    b = jnp.arange(n2, dtype=jnp.int32)
    d = jnp.arange(1, n2, dtype=jnp.int32)
    th = ((d[:, None] * b[None, :]) % n2).astype(F32) * (2.0 * math.pi / n2)
    full = jnp.concatenate([jnp.cos(th), jnp.sin(th)], axis=0) * (1.0 / math.sqrt(n))
    g, gt = full[:, :n2 - 1].astype(BF16), full[:, n2 - 1:]
    c = jnp.arange(DFT_OUTER, dtype=jnp.int32)
    tw = ((b[:, None] * c[None, :]) % n).astype(F32) * (2.0 * math.pi / n)
    return g, gt, jnp.cos(tw), jnp.sin(tw)


def _channel_dft_matrix():
    ch = jnp.arange(FN_GROUP_DIM, dtype=jnp.int32)
    th = ((ch[:, None] * ch[None, :]) % FN_GROUP_DIM).astype(F32) * (2.0 * math.pi / FN_GROUP_DIM)
    scale = 1.0 / math.sqrt(FN_GROUP_DIM)
    return jnp.concatenate([jnp.cos(th) * scale, jnp.sin(th) * scale], axis=0).astype(BF16)


def _trunk_body(x_ref, o_ref, f_ref, gmix_ref, wg_ref, wna_ref, wfn_ref, wout_ref, gmlp_ref,
                wup_ref, wdn_ref, gfin_ref, y_ref, *, ff_chunk, n_sub):
    def dot(a, b):
        return jnp.dot(a, b, preferred_element_type=F32)

    sub = x_ref.shape[1] // n_sub
    n_ff = D_FF // ff_chunk

    def chain(r):
        rows = slice(r * sub, (r + 1) * sub)
        x = x_ref[0, rows, :]
        h = _rms(x, gmix_ref[...]).astype(BF16)
        gates = dot(h, wg_ref[...])
        y_na = dot(o_ref[0, rows, :], wna_ref[...])
        y_fn = dot(f_ref[0, rows, :].astype(BF16), wfn_ref[...])
        yield
        mixed = jax.nn.sigmoid(gates[:, :D_MODEL]) * y_na + jax.nn.sigmoid(gates[:, D_MODEL:]) * y_fn
        x1 = x + dot(mixed.astype(BF16), wout_ref[...])
        yield
        h2 = _rms(x1, gmlp_ref[...]).astype(BF16)
        acc = x1
        up = dot(h2, wup_ref[:, 0:ff_chunk])
        yield
        for c in range(n_ff):
            a = jnp.maximum(up, 0.0)
            a = (a * a).astype(BF16)
            if c + 1 < n_ff:
                up = dot(h2, wup_ref[:, (c + 1) * ff_chunk:(c + 2) * ff_chunk])
            acc = acc + dot(a, wdn_ref[c * ff_chunk:(c + 1) * ff_chunk, :])
            yield
        y_ref[0, rows, :] = _rms(acc, gfin_ref[...])

    live = [chain(r) for r in range(n_sub)]
    while live:
        for g in list(live):
            try:
                next(g)
            except StopIteration:
                live.remove(g)


def _trunk(x, o, f, gmix, wg, wna, wfn, wout, gmlp, wup, wdn, gfin, tm):
    B, T, D = x.shape
    body = functools.partial(_trunk_body, ff_chunk=1024, n_sub=tm // ROW_BLOCK)

    def const(arr):
        return pl.BlockSpec(arr.shape, lambda b, i: (0, 0), pipeline_mode=pl.Buffered(1))

    return pl.pallas_call(
        body,
        grid=(B, T // tm),
        in_specs=[
            pl.BlockSpec((1, tm, D), lambda b, i: (b, i, 0)),
            pl.BlockSpec((1, tm, NA_WIDTH), lambda b, i: (b, i, 0)),
            pl.BlockSpec((1, tm, FN_WIDTH), lambda b, i: (b, i, 0)),
            const(gmix), const(wg), const(wna), const(wfn), const(wout), const(gmlp),
            const(wup), const(wdn), const(gfin),
        ],
        out_specs=pl.BlockSpec((1, tm, D), lambda b, i: (b, i, 0)),
        out_shape=jax.ShapeDtypeStruct((B, T, D), F32),
        compiler_params=pltpu.CompilerParams(
            dimension_semantics=("arbitrary", "arbitrary"), vmem_limit_bytes=VMEM_LIMIT),
        name="trunk",
    )(x, o, f, gmix, wg, wna, wfn, wout, gmlp, wup, wdn, gfin)


def _pad_rows(a, rows):
    return jnp.concatenate([a, jnp.zeros((rows - a.shape[0], a.shape[1]), a.dtype)], axis=0)


def _run_group(x, meta, consts):
    (w_qkvu, cs, bias, g_mix, wg, wna, wfn, wout, g_mlp, wup, wdn, g_fin) = consts
    B, T, _ = x.shape
    n2 = (N_META + T) // DFT_OUTER
    q, k, v, u = _inproj(x, g_mix, w_qkvu, tm=1024)
    _, km, vm, um = meta
    o = _attention(q, k, v, _pad_rows(km, GRID_W), _pad_rows(vm, GRID_W), bias, rows_per_step=64)
    f = _fourier(u, um, *_fourier_constants(n2), cs)
    return _trunk(x, o, f, g_mix, wg, wna, wfn, wout, g_mlp, wup, wdn, g_fin, tm=512)


def kernel(x_prompt, x_sample, meta_tokens, w_in, rel_bias, meta_bias, w_branch_na, w_branch_fn,
           w_out, g_mix, g_mlp, w_up, w_down, g_final):
    assert w_in.shape[0] == 1, "single-layer trunk"
    w_in0 = w_in[0]
    split = 3 * NA_WIDTH + FN_WIDTH
    w_qkvu = w_in0[:, :split].astype(BF16)
    wg = w_in0[:, split:].astype(BF16)
    cs = _channel_dft_matrix()
    bias = _bias_tables(rel_bias[0], meta_bias[0])
    g_mix2 = g_mix[0][None]
    consts = (w_qkvu, cs, bias, g_mix2, wg, w_branch_na[0].astype(BF16), w_branch_fn[0].astype(BF16),
              w_out[0].astype(BF16), g_mlp[0][None], w_up[0].astype(BF16), w_down[0].astype(BF16),
              g_final[None])
    meta = _inproj(meta_tokens[None], g_mix2, w_qkvu, tm=N_META)
    meta = tuple(m[0] for m in meta)
    return (_run_group(x_prompt, meta, consts), _run_group(x_sample, meta, consts))
```

```python
import functools
import math

import jax
import jax.numpy as jnp
import numpy as np
from jax import lax
from jax.experimental import pallas as pl
from jax.experimental.pallas import tpu as pltpu

F32 = jnp.float32
BF16 = jnp.bfloat16

D_MODEL = 1024
N_META = 16
GRID_W = 64
NA_HEADS = 8
NA_HEAD_DIM = 64
NA_WIDTH = NA_HEADS * NA_HEAD_DIM
NA_WIN_ROWS = 8
NA_WIN_COLS = 16
FN_GROUPS = 4
FN_GROUP_DIM = 128
FN_WIDTH = FN_GROUPS * FN_GROUP_DIM
D_FF = 4 * D_MODEL
RMS_EPS = 1e-6

LANES = 128
SUBLANES = 8
HEAD_PAIRS = NA_HEADS // 2
WIN_SLOTS = NA_WIN_ROWS + 2
WIN_KEYS = WIN_SLOTS * GRID_W
LOCAL_KEYS = (NA_WIN_ROWS + 1) * GRID_W
SCORES_AHEAD = 2
ROW_BLOCK = 256
NEG = -1e30
LOG2E = math.log2(math.e)
DFT_OUTER = 16
VMEM_LIMIT = 56 * 1024 * 1024


def _rms(x, g):
    ms = jnp.mean(x * x, axis=-1, keepdims=True)
    return x * lax.rsqrt(ms + RMS_EPS) * g


def _inproj_body(x_ref, g_ref, w_ref, q_ref, k_ref, v_ref, u_ref, *, n_sub):
    sub = x_ref.shape[1] // n_sub
    for r in range(n_sub):
        rows = slice(r * sub, (r + 1) * sub)
        h = _rms(x_ref[0, rows, :], g_ref[...]).astype(BF16)
        z = jnp.dot(h, w_ref[...], preferred_element_type=F32)
        q_ref[0, rows, :] = (z[:, :NA_WIDTH] * (NA_HEAD_DIM ** -0.5 * LOG2E)).astype(BF16)
        k_ref[0, rows, :] = z[:, NA_WIDTH:2 * NA_WIDTH].astype(BF16)
        v_ref[0, rows, :] = z[:, 2 * NA_WIDTH:3 * NA_WIDTH].astype(BF16)
        u_ref[0, rows, :] = z[:, 3 * NA_WIDTH:]


def _inproj(x, g, w, tm):
    B, T, D = x.shape
    tok = pl.BlockSpec((1, tm, NA_WIDTH), lambda b, i: (b, i, 0))
    return pl.pallas_call(
        functools.partial(_inproj_body, n_sub=max(tm // ROW_BLOCK, 1)),
        grid=(B, T // tm),
        in_specs=[
            pl.BlockSpec((1, tm, D), lambda b, i: (b, i, 0)),
            pl.BlockSpec((1, D), lambda b, i: (0, 0)),
            pl.BlockSpec(w.shape, lambda b, i: (0, 0)),
        ],
        out_specs=[tok] * 4,
        out_shape=[jax.ShapeDtypeStruct((B, T, NA_WIDTH), BF16)] * 3
        + [jax.ShapeDtypeStruct((B, T, FN_WIDTH), F32)],
        compiler_params=pltpu.CompilerParams(
            dimension_semantics=("arbitrary", "arbitrary"), vmem_limit_bytes=VMEM_LIMIT),
        name="inproj",
    )(x, g, w)


def _attn_body(q_ref, k_ref, v_ref, km_ref, vm_ref, bias_ref, o_ref, *, rows, rows_per_step):
    chunk = pl.program_id(2)
    first_head = lax.broadcasted_iota(jnp.int32, (2 * GRID_W, LANES), 1) < NA_HEAD_DIM
    first_head_kv = lax.broadcasted_iota(jnp.int32, (WIN_KEYS, LANES), 1) < NA_HEAD_DIM
    km = km_ref[...]
    vm = vm_ref[...]
    zero = jnp.zeros((2 * GRID_W, LANES), BF16)
    ones = jnp.ones((WIN_KEYS, LANES), BF16)

    def scores(i):
        r = chunk * rows_per_step + 2 * i
        start = jnp.clip(r - NA_WIN_ROWS // 2, 0, rows - (NA_WIN_ROWS + 1))
        cfg = jnp.where(r == 0, 0, jnp.where(r == 2, 1, jnp.where(
            r == rows - 4, 3, jnp.where(r == rows - 2, 4, 2))))
        q2 = q_ref[0, i * 2 * GRID_W:(i + 1) * 2 * GRID_W, :]
        qq = jnp.concatenate([jnp.where(first_head, q2, zero), jnp.where(first_head, zero, q2)], axis=0)
        koff = pl.multiple_of(start * GRID_W, GRID_W)
        kw = jnp.concatenate([k_ref[0, pl.ds(koff, LOCAL_KEYS), :], km], axis=0)
        s = lax.dot_general(qq, kw, (((1,), (1,)), ((), ())), preferred_element_type=F32)
        bias = jnp.concatenate(
            [jnp.concatenate([bias_ref[cfg, row, sp, 0, head] for sp in range(WIN_SLOTS // 2)], axis=1)
             for head in range(2) for row in range(2)], axis=0)
        return s + bias, koff

    def finish(i, s, koff):
        vw = jnp.concatenate([v_ref[0, pl.ds(koff, LOCAL_KEYS), :], vm], axis=0)
        v0 = jnp.where(first_head_kv, vw, ones)
        v1 = jnp.where(first_head_kv, ones, vw)
        m = jnp.max(s, axis=-1, keepdims=True)
        p = jnp.exp2(s - m).astype(BF16)
        o0 = jnp.dot(p[:2 * GRID_W], v0, preferred_element_type=F32)
        o1 = jnp.dot(p[2 * GRID_W:], v1, preferred_element_type=F32)
        r0 = o0 / pltpu.roll(o0, NA_HEAD_DIM, axis=1)
        r1 = o1 / pltpu.roll(o1, NA_HEAD_DIM, axis=1)
        o_ref[0, i * 2 * GRID_W:(i + 1) * 2 * GRID_W, :] = jnp.where(first_head, r0, r1).astype(BF16)

    n_pairs = rows_per_step // 2
    queue = [scores(i) for i in range(min(SCORES_AHEAD, n_pairs))]
    for i in range(n_pairs):
        if i + SCORES_AHEAD < n_pairs:
            queue.append(scores(i + SCORES_AHEAD))
        finish(i, *queue.pop(0))


def _attention(q, k, v, km, vm, bias, rows_per_step):
    B, T, _ = q.shape
    rows = T // GRID_W
    tq = rows_per_step * GRID_W
    body = functools.partial(_attn_body, rows=rows, rows_per_step=rows_per_step)
    return pl.pallas_call(
        body,
        grid=(B, HEAD_PAIRS, rows // rows_per_step),
        in_specs=[
            pl.BlockSpec((1, tq, LANES), lambda b, h, c: (b, c, h)),
            pl.BlockSpec((1, T, LANES), lambda b, h, c: (b, 0, h)),
            pl.BlockSpec((1, T, LANES), lambda b, h, c: (b, 0, h)),
            pl.BlockSpec((GRID_W, LANES), lambda b, h, c: (0, h)),
            pl.BlockSpec((GRID_W, LANES), lambda b, h, c: (0, h)),
            pl.BlockSpec(bias.shape[:3] + (1,) + bias.shape[4:], lambda b, h, c: (0, 0, 0, h, 0, 0, 0)),
        ],
        out_specs=pl.BlockSpec((1, tq, LANES), lambda b, h, c: (b, c, h)),
        out_shape=jax.ShapeDtypeStruct((B, T, NA_WIDTH), BF16),
        compiler_params=pltpu.CompilerParams(
            dimension_semantics=("arbitrary", "arbitrary", "arbitrary"), vmem_limit_bytes=VMEM_LIMIT),
        name="attention",
    )(q, k, v, km, vm, bias)


PAIR_CONFIGS = (((0, 1), ((0, 8), (0, 8))), ((2, 3), ((0, 8), (0, 8))), ((4, 5), ((0, 8), (1, 9))),
                ((5, 6), ((1, 9), (1, 9))), ((7, 8), ((1, 9), (1, 9))))


def _bias_tables(rel_bias, meta_bias):
    c = np.arange(GRID_W)[:, None]
    j = np.arange(GRID_W)[None, :]
    cs = np.clip(c - NA_WIN_COLS // 2, 0, GRID_W - NA_WIN_COLS)
    valid = (j >= cs) & (j < cs + NA_WIN_COLS)
    n_off = 2 * NA_WIN_COLS - 1
    n_row_off = 2 * NA_WIN_ROWS - 1
    onehot = (valid[None] & ((j - c + NA_WIN_COLS - 1)[None] == np.arange(n_off)[:, None, None])).astype(np.float32)
    hi = lax.Precision.HIGHEST
    toe = jnp.einsum("hrx,xcj->rhcj", rel_bias * LOG2E, onehot, precision=hi)
    toe = toe + np.where(valid, 0.0, NEG).astype(np.float32)
    negblk = jnp.full((1, NA_HEADS, GRID_W, GRID_W), NEG, F32)
    metablk = jnp.concatenate(
        [jnp.broadcast_to(meta_bias[:, None, :] * LOG2E, (NA_HEADS, GRID_W, N_META)),
         jnp.full((NA_HEADS, GRID_W, GRID_W - N_META), NEG, F32)], axis=-1)[None]
    blocks = jnp.concatenate([toe, negblk, metablk], axis=0)
    select = np.zeros((len(PAIR_CONFIGS), 2, WIN_SLOTS, n_row_off + 2), np.float32)
    for kk, (qslots, wins) in enumerate(PAIR_CONFIGS):
        for i, (a, (w0, w1)) in enumerate(zip(qslots, wins)):
            for w in range(WIN_SLOTS):
                if w == WIN_SLOTS - 1:
                    src = n_row_off + 1
                else:
                    src = w - a + NA_WIN_ROWS - 1 if w0 <= w < w1 else n_row_off
                select[kk, i, w, src] = 1.0
    even = jnp.einsum("kiwr,rhcj->kiwhcj", select[:, :, 0::2], blocks, precision=hi)
    odd = jnp.einsum("kiwr,rhcj->kiwhcj", select[:, :, 1::2], blocks, precision=hi)
    t = jnp.concatenate([even, odd], axis=-1)
    return t.reshape(t.shape[:3] + (HEAD_PAIRS, 2, GRID_W, 2 * GRID_W))


def _snap(x):
    for v in (0.0, 1.0, -1.0):
        if abs(x - v) < 1e-12:
            return v
    return x


def _scaled(x, coef):
    if x is None or coef == 0.0:
        return None
    if coef == 1.0:
        return x
    if coef == -1.0:
        return -x
    return x * coef


def _plus(a, b):
    if a is None:
        return b
    if b is None:
        return a
    return a + b


def _minus(a, b):
    if b is None:
        return a
    if a is None:
        return -b
    return a - b


def _fft_dit(zs, keep=None):
    n = len(zs)
    if n == 1:
        return zs
    even, odd = _fft_dit(zs[0::2]), _fft_dit(zs[1::2])
    out = [None] * n
    for kk in range(n // 2):
        c, s = _snap(math.cos(2 * math.pi * kk / n)), _snap(math.sin(2 * math.pi * kk / n))
        orr, oi = odd[kk]
        tr = _plus(_scaled(orr, c), _scaled(oi, s))
        ti = _plus(_scaled(oi, c), _scaled(orr, -s))
        er, ei = even[kk]
        if keep is None or kk in keep:
            out[kk] = (_plus(er, tr), _plus(ei, ti))
        if keep is None or kk + n // 2 in keep:
            out[kk + n // 2] = (_minus(er, tr), _minus(ei, ti))
    return out


def _dft16_real(xs):
    half = DFT_OUTER // 2
    return _fft_dit([(x, None) for x in xs], keep=set(range(half + 1)))[:half + 1]


def _fourier_body(u_ref, um_ref, g_ref, gt_ref, twc_ref, tws_ref, cs_ref, f_ref,
                  zr_ref, zi_ref, sh_ref, *, n2):
    nm = n2 - 1
    half = DFT_OUTER // 2
    def stage1(r0, meta_rows):
        xs = [um_ref[pl.ds(r0, SUBLANES), :] if (a == 0 and meta_rows)
              else u_ref[0, pl.ds(n2 * a - N_META + r0, SUBLANES), :] for a in range(DFT_OUTER)]
        for c, (zr, zi) in enumerate(_dft16_real(xs)):
            zr_ref[c, pl.ds(r0, SUBLANES), :] = zr
            if zi is not None:
                zi_ref[c, pl.ds(r0, SUBLANES), :] = zi

    for i in range(N_META // SUBLANES):
        stage1(i * SUBLANES, True)

    def stage1_step(i, carry):
        stage1(pl.multiple_of(i * SUBLANES, SUBLANES), False)
        return carry

    lax.fori_loop(N_META // SUBLANES, nm // SUBLANES, stage1_step, 0, unroll=2)
    zt = _dft16_real([u_ref[0, pl.ds(n2 * a + nm - N_META, 1), :] for a in range(DFT_OUTER)])

    g = g_ref[...]
    gt = gt_ref[...]
    csm = cs_ref[...]
    inv_sqrt_n = 1.0 / math.sqrt(DFT_OUTER * n2)

    def prep(s):
        zr = zr_ref[s]
        ztr, zti = zt[s]
        zi = None if s in (0, half) else zi_ref[s]
        if s > 0:
            tc, ts = twc_ref[0:nm, s:s + 1], tws_ref[0:nm, s:s + 1]
            tct, tst = twc_ref[nm:n2, s:s + 1], tws_ref[nm:n2, s:s + 1]
            if zi is None:
                zr, zi = zr * tc, -(zr * ts)
                ztr, zti = ztr * tct, -(ztr * tst)
            else:
                zr, zi = zr * tc + zi * ts, zi * tc - zr * ts
                ztr, zti = ztr * tct + zti * tst, zti * tct - ztr * tst
        if zi is None:
            zi, zti = jnp.zeros_like(zr), jnp.zeros_like(ztr)
        z = jnp.concatenate([zr, zi], axis=1)
        tail = jnp.concatenate([ztr, zti], axis=1)
        return z, tail

    def dense(z, tail):
        p = jnp.dot(g, z.astype(BF16), preferred_element_type=F32)
        return p + gt * tail

    def channel_store(c, pr, pi):
        y = jnp.dot(jnp.concatenate([pr, pi], axis=1).astype(BF16), csm, preferred_element_type=F32)
        f_ref[0, pl.ds(c, nm, stride=DFT_OUTER), :] = y

    def post(s, p, z, tail):
        czr, czi, szr, szi = p[:nm, :LANES], p[:nm, LANES:], p[nm:, :LANES], p[nm:, LANES:]
        channel_store(s, czr + szi, czi - szr)
        if 0 < s < half:
            col = (jnp.sum(z, axis=0, keepdims=True) + tail) * inv_sqrt_n
            sh_ref[0, 0:nm] = czr - szi
            sh_ref[1, 0:nm] = -(czi + szr)
            sh_ref[0, nm:n2] = col[:, :LANES]
            sh_ref[1, nm:n2] = -col[:, LANES:]
            channel_store(DFT_OUTER - s, sh_ref[0, 1:n2], sh_ref[1, 1:n2])

    pending = None
    for s in range(half + 1):
        z, tail = prep(s)
        p = dense(z, tail)
        if pending is not None:
            post(*pending)
        pending = (s, p, z, tail)
    post(*pending)


def _fourier(u, um, g, gt, twc, tws, cs):
    B, T, _ = u.shape
    n2 = (N_META + T) // DFT_OUTER
    nm = n2 - 1
    body = functools.partial(_fourier_body, n2=n2)

    def const(arr):
        return pl.BlockSpec(arr.shape, lambda b, j: (0, 0))

    return pl.pallas_call(
        body,
        grid=(B, FN_GROUPS),
        in_specs=[
            pl.BlockSpec((1, T, LANES), lambda b, j: (b, 0, j)),
            pl.BlockSpec((N_META, LANES), lambda b, j: (0, j)),
            const(g), const(gt), const(twc), const(tws), const(cs),
        ],
        out_specs=pl.BlockSpec((1, T, LANES), lambda b, j: (b, 0, j)),
        out_shape=jax.ShapeDtypeStruct((B, T, FN_WIDTH), F32),
        scratch_shapes=[
            pltpu.VMEM((DFT_OUTER // 2 + 1, nm, LANES), F32),
            pltpu.VMEM((DFT_OUTER // 2 + 1, nm, LANES), F32),
            pltpu.VMEM((2, nm + SUBLANES, LANES), F32),
        ],
        compiler_params=pltpu.CompilerParams(
            dimension_semantics=("arbitrary", "arbitrary"), vmem_limit_bytes=VMEM_LIMIT),
        name="fourier",
    )(u, um, g, gt, twc, tws, cs)


def _fourier_constants(n2):
    n = DFT_OUTER * n2
    b = jnp.arange(n2, dtype=jnp.int32)
    d = jnp.arange(1, n2, dtype=jnp.int32)
    th = ((d[:, None] * b[None, :]) % n2).astype(F32) * (2.0 * math.pi / n2)
    full = jnp.concatenate([jnp.cos(th), jnp.sin(th)], axis=0) * (1.0 / math.sqrt(n))
    g, gt = full[:, :n2 - 1].astype(BF16), full[:, n2 - 1:]
    c = jnp.arange(DFT_OUTER, dtype=jnp.int32)
    tw = ((b[:, None] * c[None, :]) % n).astype(F32) * (2.0 * math.pi / n)
    return g, gt, jnp.cos(tw), jnp.sin(tw)


def _channel_dft_matrix():
    ch = jnp.arange(FN_GROUP_DIM, dtype=jnp.int32)
    th = ((ch[:, None] * ch[None, :]) % FN_GROUP_DIM).astype(F32) * (2.0 * math.pi / FN_GROUP_DIM)
    scale = 1.0 / math.sqrt(FN_GROUP_DIM)
    return jnp.concatenate([jnp.cos(th) * scale, jnp.sin(th) * scale], axis=0).astype(BF16)


def _trunk_body(x_ref, o_ref, f_ref, gmix_ref, wg_ref, wna_ref, wfn_ref, wout_ref, gmlp_ref,
                wup_ref, wdn_ref, gfin_ref, y_ref, *, ff_chunk, n_sub):
    def dot(a, b):
        return jnp.dot(a, b, preferred_element_type=F32)

    sub = x_ref.shape[1] // n_sub
    n_ff = D_FF // ff_chunk

    def chain(r):
        rows = slice(r * sub, (r + 1) * sub)
        x = x_ref[0, rows, :]
        h = _rms(x, gmix_ref[...]).astype(BF16)
        gates = dot(h, wg_ref[...])
        y_na = dot(o_ref[0, rows, :], wna_ref[...])
        y_fn = dot(f_ref[0, rows, :].astype(BF16), wfn_ref[...])
        yield
        mixed = jax.nn.sigmoid(gates[:, :D_MODEL]) * y_na + jax.nn.sigmoid(gates[:, D_MODEL:]) * y_fn
        x1 = x + dot(mixed.astype(BF16), wout_ref[...])
        yield
        h2 = _rms(x1, gmlp_ref[...]).astype(BF16)
        acc = x1
        up = dot(h2, wup_ref[:, 0:ff_chunk])
        yield
        for c in range(n_ff):
            a = jnp.maximum(up, 0.0)
            a = (a * a).astype(BF16)
            if c + 1 < n_ff:
                up = dot(h2, wup_ref[:, (c + 1) * ff_chunk:(c + 2) * ff_chunk])
            acc = acc + dot(a, wdn_ref[c * ff_chunk:(c + 1) * ff_chunk, :])
            yield
        y_ref[0, rows, :] = _rms(acc, gfin_ref[...])

    live = [chain(r) for r in range(n_sub)]
    while live:
        for g in list(live):
            try:
                next(g)
            except StopIteration:
                live.remove(g)


def _trunk(x, o, f, gmix, wg, wna, wfn, wout, gmlp, wup, wdn, gfin, tm):
    B, T, D = x.shape
    body = functools.partial(_trunk_body, ff_chunk=1024, n_sub=tm // ROW_BLOCK)

    def const(arr):
        return pl.BlockSpec(arr.shape, lambda b, i: (0, 0), pipeline_mode=pl.Buffered(1))

    return pl.pallas_call(
        body,
        grid=(B, T // tm),
        in_specs=[
            pl.BlockSpec((1, tm, D), lambda b, i: (b, i, 0)),
            pl.BlockSpec((1, tm, NA_WIDTH), lambda b, i: (b, i, 0)),
            pl.BlockSpec((1, tm, FN_WIDTH), lambda b, i: (b, i, 0)),
            const(gmix), const(wg), const(wna), const(wfn), const(wout), const(gmlp),
            const(wup), const(wdn), const(gfin),
        ],
        out_specs=pl.BlockSpec((1, tm, D), lambda b, i: (b, i, 0)),
        out_shape=jax.ShapeDtypeStruct((B, T, D), F32),
        compiler_params=pltpu.CompilerParams(
            dimension_semantics=("arbitrary", "arbitrary"), vmem_limit_bytes=VMEM_LIMIT),
        name="trunk",
    )(x, o, f, gmix, wg, wna, wfn, wout, gmlp, wup, wdn, gfin)


def _pad_rows(a, rows):
    return jnp.concatenate([a, jnp.zeros((rows - a.shape[0], a.shape[1]), a.dtype)], axis=0)


def _run_group(x, meta, consts):
    (w_qkvu, cs, bias, g_mix, wg, wna, wfn, wout, g_mlp, wup, wdn, g_fin) = consts
    B, T, _ = x.shape
    n2 = (N_META + T) // DFT_OUTER
    q, k, v, u = _inproj(x, g_mix, w_qkvu, tm=1024)
    _, km, vm, um = meta
    o = _attention(q, k, v, _pad_rows(km, GRID_W), _pad_rows(vm, GRID_W), bias, rows_per_step=64)
    f = _fourier(u, um, *_fourier_constants(n2), cs)
    return _trunk(x, o, f, g_mix, wg, wna, wfn, wout, g_mlp, wup, wdn, g_fin, tm=512)


def kernel(x_prompt, x_sample, meta_tokens, w_in, rel_bias, meta_bias, w_branch_na, w_branch_fn,
           w_out, g_mix, g_mlp, w_up, w_down, g_final):
    assert w_in.shape[0] == 1, "single-layer trunk"
    w_in0 = w_in[0]
    split = 3 * NA_WIDTH + FN_WIDTH
    w_qkvu = w_in0[:, :split].astype(BF16)
    wg = w_in0[:, split:].astype(BF16)
    cs = _channel_dft_matrix()
    bias = _bias_tables(rel_bias[0], meta_bias[0])
    g_mix2 = g_mix[0][None]
    consts = (w_qkvu, cs, bias, g_mix2, wg, w_branch_na[0].astype(BF16), w_branch_fn[0].astype(BF16),
              w_out[0].astype(BF16), g_mlp[0][None], w_up[0].astype(BF16), w_down[0].astype(BF16),
              g_final[None])
    meta = _inproj(meta_tokens[None], g_mix2, w_qkvu, tm=N_META)
    meta = tuple(m[0] for m in meta)
    return (_run_group(x_prompt, meta, consts), _run_group(x_sample, meta, consts))
```

```python
import functools
import math

import jax
import jax.numpy as jnp
import numpy as np
from jax import lax
from jax.experimental import pallas as pl
from jax.experimental.pallas import tpu as pltpu

F32 = jnp.float32
BF16 = jnp.bfloat16

D_MODEL = 1024
N_META = 16
GRID_W = 64
NA_HEADS = 8
NA_HEAD_DIM = 64
NA_WIDTH = NA_HEADS * NA_HEAD_DIM
NA_WIN_ROWS = 8
NA_WIN_COLS = 16
FN_GROUPS = 4
FN_GROUP_DIM = 128
FN_WIDTH = FN_GROUPS * FN_GROUP_DIM
D_FF = 4 * D_MODEL
RMS_EPS = 1e-6

LANES = 128
SUBLANES = 8
HEAD_PAIRS = NA_HEADS // 2
WIN_SLOTS = NA_WIN_ROWS + 2
WIN_KEYS = WIN_SLOTS * GRID_W
LOCAL_KEYS = (NA_WIN_ROWS + 1) * GRID_W
SCORES_AHEAD = 2
ROW_BLOCK = 256
NEG = -1e30
LOG2E = math.log2(math.e)
DFT_OUTER = 16
VMEM_LIMIT = 56 * 1024 * 1024


def _rms(x, g):
    ms = jnp.mean(x * x, axis=-1, keepdims=True)
    return x * lax.rsqrt(ms + RMS_EPS) * g


def _inproj_body(x_ref, g_ref, w_ref, q_ref, k_ref, v_ref, u_ref, *, n_sub):
    sub = x_ref.shape[1] // n_sub
    for r in range(n_sub):
        rows = slice(r * sub, (r + 1) * sub)
        h = _rms(x_ref[0, rows, :], g_ref[...]).astype(BF16)
        z = jnp.dot(h, w_ref[...], preferred_element_type=F32)
        q_ref[0, rows, :] = (z[:, :NA_WIDTH] * (NA_HEAD_DIM ** -0.5 * LOG2E)).astype(BF16)
        k_ref[0, rows, :] = z[:, NA_WIDTH:2 * NA_WIDTH].astype(BF16)
        v_ref[0, rows, :] = z[:, 2 * NA_WIDTH:3 * NA_WIDTH].astype(BF16)
        u_ref[0, rows, :] = z[:, 3 * NA_WIDTH:]


def _inproj(x, g, w, tm):
    B, T, D = x.shape
    tok = pl.BlockSpec((1, tm, NA_WIDTH), lambda b, i: (b, i, 0))
    return pl.pallas_call(
        functools.partial(_inproj_body, n_sub=max(tm // ROW_BLOCK, 1)),
        grid=(B, T // tm),
        in_specs=[
            pl.BlockSpec((1, tm, D), lambda b, i: (b, i, 0)),
            pl.BlockSpec((1, D), lambda b, i: (0, 0)),
            pl.BlockSpec(w.shape, lambda b, i: (0, 0)),
        ],
        out_specs=[tok] * 4,
        out_shape=[jax.ShapeDtypeStruct((B, T, NA_WIDTH), BF16)] * 3
        + [jax.ShapeDtypeStruct((B, T, FN_WIDTH), F32)],
        compiler_params=pltpu.CompilerParams(
            dimension_semantics=("arbitrary", "arbitrary"), vmem_limit_bytes=VMEM_LIMIT),
        name="inproj",
    )(x, g, w)


def _attn_body(q_ref, k_ref, v_ref, km_ref, vm_ref, bias_ref, o_ref, *, rows, rows_per_step):
    chunk = pl.program_id(2)
    first_head = lax.broadcasted_iota(jnp.int32, (2 * GRID_W, LANES), 1) < NA_HEAD_DIM
    first_head_kv = lax.broadcasted_iota(jnp.int32, (WIN_KEYS, LANES), 1) < NA_HEAD_DIM
    km = km_ref[...]
    vm = vm_ref[...]
    zero = jnp.zeros((2 * GRID_W, LANES), BF16)
    ones = jnp.ones((WIN_KEYS, LANES), BF16)

    def scores(i):
        r = chunk * rows_per_step + 2 * i
        start = jnp.clip(r - NA_WIN_ROWS // 2, 0, rows - (NA_WIN_ROWS + 1))
        cfg = jnp.where(r == 0, 0, jnp.where(r == 2, 1, jnp.where(
            r == rows - 4, 3, jnp.where(r == rows - 2, 4, 2))))
        q2 = q_ref[0, i * 2 * GRID_W:(i + 1) * 2 * GRID_W, :]
        qq = jnp.concatenate([jnp.where(first_head, q2, zero), jnp.where(first_head, zero, q2)], axis=0)
        koff = pl.multiple_of(start * GRID_W, GRID_W)
        kw = jnp.concatenate([k_ref[0, pl.ds(koff, LOCAL_KEYS), :], km], axis=0)
        s = lax.dot_general(qq, kw, (((1,), (1,)), ((), ())), preferred_element_type=F32)
        bias = jnp.concatenate(
            [jnp.concatenate([bias_ref[cfg, row, sp, 0, head] for sp in range(WIN_SLOTS // 2)], axis=1)
             for head in range(2) for row in range(2)], axis=0)
        return s + bias, koff

    def finish(i, s, koff):
        vw = jnp.concatenate([v_ref[0, pl.ds(koff, LOCAL_KEYS), :], vm], axis=0)
        v0 = jnp.where(first_head_kv, vw, ones)
        v1 = jnp.where(first_head_kv, ones, vw)
        m = jnp.max(s, axis=-1, keepdims=True)
        p = jnp.exp2(s - m).astype(BF16)
        o0 = jnp.dot(p[:2 * GRID_W], v0, preferred_element_type=F32)
        o1 = jnp.dot(p[2 * GRID_W:], v1, preferred_element_type=F32)
        r0 = o0 / pltpu.roll(o0, NA_HEAD_DIM, axis=1)
        r1 = o1 / pltpu.roll(o1, NA_HEAD_DIM, axis=1)
        o_ref[0, i * 2 * GRID_W:(i + 1) * 2 * GRID_W, :] = jnp.where(first_head, r0, r1).astype(BF16)

    n_pairs = rows_per_step // 2
    queue = [scores(i) for i in range(min(SCORES_AHEAD, n_pairs))]
    for i in range(n_pairs):
        if i + SCORES_AHEAD < n_pairs:
            queue.append(scores(i + SCORES_AHEAD))
        finish(i, *queue.pop(0))


def _attention(q, k, v, km, vm, bias, rows_per_step):
    B, T, _ = q.shape
    rows = T // GRID_W
    tq = rows_per_step * GRID_W
    body = functools.partial(_attn_body, rows=rows, rows_per_step=rows_per_step)
    return pl.pallas_call(
        body,
        grid=(B, HEAD_PAIRS, rows // rows_per_step),
        in_specs=[
            pl.BlockSpec((1, tq, LANES), lambda b, h, c: (b, c, h)),
            pl.BlockSpec((1, T, LANES), lambda b, h, c: (b, 0, h)),
            pl.BlockSpec((1, T, LANES), lambda b, h, c: (b, 0, h)),
            pl.BlockSpec((GRID_W, LANES), lambda b, h, c: (0, h)),
            pl.BlockSpec((GRID_W, LANES), lambda b, h, c: (0, h)),
            pl.BlockSpec(bias.shape[:3] + (1,) + bias.shape[4:], lambda b, h, c: (0, 0, 0, h, 0, 0, 0)),
        ],
        out_specs=pl.BlockSpec((1, tq, LANES), lambda b, h, c: (b, c, h)),
        out_shape=jax.ShapeDtypeStruct((B, T, NA_WIDTH), BF16),
        compiler_params=pltpu.CompilerParams(
            dimension_semantics=("arbitrary", "arbitrary", "arbitrary"), vmem_limit_bytes=VMEM_LIMIT),
        name="attention",
    )(q, k, v, km, vm, bias)


PAIR_CONFIGS = (((0, 1), ((0, 8), (0, 8))), ((2, 3), ((0, 8), (0, 8))), ((4, 5), ((0, 8), (1, 9))),
                ((5, 6), ((1, 9), (1, 9))), ((7, 8), ((1, 9), (1, 9))))


def _bias_tables(rel_bias, meta_bias):
    c = np.arange(GRID_W)[:, None]
    j = np.arange(GRID_W)[None, :]
    cs = np.clip(c - NA_WIN_COLS // 2, 0, GRID_W - NA_WIN_COLS)
    valid = (j >= cs) & (j < cs + NA_WIN_COLS)
    n_off = 2 * NA_WIN_COLS - 1
    n_row_off = 2 * NA_WIN_ROWS - 1
    onehot = (valid[None] & ((j - c + NA_WIN_COLS - 1)[None] == np.arange(n_off)[:, None, None])).astype(np.float32)
    hi = lax.Precision.HIGHEST
    toe = jnp.einsum("hrx,xcj->rhcj", rel_bias * LOG2E, onehot, precision=hi)
    toe = toe + np.where(valid, 0.0, NEG).astype(np.float32)
    negblk = jnp.full((1, NA_HEADS, GRID_W, GRID_W), NEG, F32)
    metablk = jnp.concatenate(
        [jnp.broadcast_to(meta_bias[:, None, :] * LOG2E, (NA_HEADS, GRID_W, N_META)),
         jnp.full((NA_HEADS, GRID_W, GRID_W - N_META), NEG, F32)], axis=-1)[None]
    blocks = jnp.concatenate([toe, negblk, metablk], axis=0)
    source = np.zeros((len(PAIR_CONFIGS), 2, WIN_SLOTS), np.int32)
    for kk, (qslots, wins) in enumerate(PAIR_CONFIGS):
        for i, (a, (w0, w1)) in enumerate(zip(qslots, wins)):
            for w in range(WIN_SLOTS - 1):
                source[kk, i, w] = w - a + NA_WIN_ROWS - 1 if w0 <= w < w1 else n_row_off
            source[kk, i, WIN_SLOTS - 1] = n_row_off + 1
    zeros = jnp.zeros_like(blocks)
    left = jnp.concatenate([blocks, zeros], axis=-1)
    right = jnp.concatenate([zeros, blocks], axis=-1)

    def assemble(left_ref, right_ref, out_ref):
        for kk in range(source.shape[0]):
            for i in range(2):
                for sp in range(WIN_SLOTS // 2):
                    out_ref[kk, i, sp] = (left_ref[int(source[kk, i, 2 * sp])]
                                          + right_ref[int(source[kk, i, 2 * sp + 1])])

    t = pl.pallas_call(
        assemble,
        out_shape=jax.ShapeDtypeStruct(source.shape[:2] + (WIN_SLOTS // 2, NA_HEADS, GRID_W, 2 * GRID_W), F32),
        compiler_params=pltpu.CompilerParams(vmem_limit_bytes=VMEM_LIMIT),
        name="bias_assemble",
    )(left, right)
    return t.reshape(t.shape[:3] + (HEAD_PAIRS, 2, GRID_W, 2 * GRID_W))


def _snap(x):
    for v in (0.0, 1.0, -1.0):
        if abs(x - v) < 1e-12:
            return v
    return x


def _scaled(x, coef):
    if x is None or coef == 0.0:
        return None
    if coef == 1.0:
        return x
    if coef == -1.0:
        return -x
    return x * coef


def _plus(a, b):
    if a is None:
        return b
    if b is None:
        return a
    return a + b


def _minus(a, b):
    if b is None:
        return a
    if a is None:
        return -b
    return a - b


def _fft_dit(zs, keep=None):
    n = len(zs)
    if n == 1:
        return zs
    even, odd = _fft_dit(zs[0::2]), _fft_dit(zs[1::2])
    out = [None] * n
    for kk in range(n // 2):
        c, s = _snap(math.cos(2 * math.pi * kk / n)), _snap(math.sin(2 * math.pi * kk / n))
        orr, oi = odd[kk]
        tr = _plus(_scaled(orr, c), _scaled(oi, s))
        ti = _plus(_scaled(oi, c), _scaled(orr, -s))
        er, ei = even[kk]
        if keep is None or kk in keep:
            out[kk] = (_plus(er, tr), _plus(ei, ti))
        if keep is None or kk + n // 2 in keep:
            out[kk + n // 2] = (_minus(er, tr), _minus(ei, ti))
    return out


def _dft16_real(xs):
    half = DFT_OUTER // 2
    return _fft_dit([(x, None) for x in xs], keep=set(range(half + 1)))[:half + 1]


def _fourier_body(u_ref, um_ref, g_ref, gt_ref, twc_ref, tws_ref, cs_ref, f_ref,
                  zr_ref, zi_ref, sh_ref, *, n2):
    nm = n2 - 1
    half = DFT_OUTER // 2
    def stage1(r0, meta_rows):
        xs = [um_ref[pl.ds(r0, SUBLANES), :] if (a == 0 and meta_rows)
              else u_ref[0, pl.ds(n2 * a - N_META + r0, SUBLANES), :] for a in range(DFT_OUTER)]
        for c, (zr, zi) in enumerate(_dft16_real(xs)):
            zr_ref[c, pl.ds(r0, SUBLANES), :] = zr
            if zi is not None:
                zi_ref[c, pl.ds(r0, SUBLANES), :] = zi

    for i in range(N_META // SUBLANES):
        stage1(i * SUBLANES, True)

    def stage1_step(i, carry):
        stage1(pl.multiple_of(i * SUBLANES, SUBLANES), False)
        return carry

    lax.fori_loop(N_META // SUBLANES, nm // SUBLANES, stage1_step, 0, unroll=2)
    zt = _dft16_real([u_ref[0, pl.ds(n2 * a + nm - N_META, 1), :] for a in range(DFT_OUTER)])

    g = g_ref[...]
    gt = gt_ref[...]
    csm = cs_ref[...]
    inv_sqrt_n = 1.0 / math.sqrt(DFT_OUTER * n2)

    def prep(s):
        zr = zr_ref[s]
        ztr, zti = zt[s]
        zi = None if s in (0, half) else zi_ref[s]
        if s > 0:
            tc, ts = twc_ref[0:nm, s:s + 1], tws_ref[0:nm, s:s + 1]
            tct, tst = twc_ref[nm:n2, s:s + 1], tws_ref[nm:n2, s:s + 1]
            if zi is None:
                zr, zi = zr * tc, -(zr * ts)
                ztr, zti = ztr * tct, -(ztr * tst)
            else:
                zr, zi = zr * tc + zi * ts, zi * tc - zr * ts
                ztr, zti = ztr * tct + zti * tst, zti * tct - ztr * tst
        if zi is None:
            zi, zti = jnp.zeros_like(zr), jnp.zeros_like(ztr)
        z = jnp.concatenate([zr, zi], axis=1)
        tail = jnp.concatenate([ztr, zti], axis=1)
        return z, tail

    def dense(z, tail):
        p = jnp.dot(g, z.astype(BF16), preferred_element_type=F32)
        return p + gt * tail

    def channel_store(c, pr, pi):
        y = jnp.dot(jnp.concatenate([pr, pi], axis=1).astype(BF16), csm, preferred_element_type=F32)
        f_ref[0, pl.ds(c, nm, stride=DFT_OUTER), :] = y

    def post(s, p, z, tail):
        czr, czi, szr, szi = p[:nm, :LANES], p[:nm, LANES:], p[nm:, :LANES], p[nm:, LANES:]
        channel_store(s, czr + szi, czi - szr)
        if 0 < s < half:
            col = (jnp.sum(z, axis=0, keepdims=True) + tail) * inv_sqrt_n
            sh_ref[0, 0:nm] = czr - szi
            sh_ref[1, 0:nm] = -(czi + szr)
            sh_ref[0, nm:n2] = col[:, :LANES]
            sh_ref[1, nm:n2] = -col[:, LANES:]
            channel_store(DFT_OUTER - s, sh_ref[0, 1:n2], sh_ref[1, 1:n2])

    pending = None
    for s in range(half + 1):
        z, tail = prep(s)
        p = dense(z, tail)
        if pending is not None:
            post(*pending)
        pending = (s, p, z, tail)
    post(*pending)


def _fourier(u, um, g, gt, twc, tws, cs):
    B, T, _ = u.shape
    n2 = (N_META + T) // DFT_OUTER
    nm = n2 - 1
    body = functools.partial(_fourier_body, n2=n2)

    def const(arr):
        return pl.BlockSpec(arr.shape, lambda b, j: (0, 0))

    return pl.pallas_call(
        body,
        grid=(B, FN_GROUPS),
        in_specs=[
            pl.BlockSpec((1, T, LANES), lambda b, j: (b, 0, j)),
            pl.BlockSpec((N_META, LANES), lambda b, j: (0, j)),
            const(g), const(gt), const(twc), const(tws), const(cs),
        ],
        out_specs=pl.BlockSpec((1, T, LANES), lambda b, j: (b, 0, j)),
        out_shape=jax.ShapeDtypeStruct((B, T, FN_WIDTH), F32),
        scratch_shapes=[
            pltpu.VMEM((DFT_OUTER // 2 + 1, nm, LANES), F32),
            pltpu.VMEM((DFT_OUTER // 2 + 1, nm, LANES), F32),
            pltpu.VMEM((2, nm + SUBLANES, LANES), F32),
        ],
        compiler_params=pltpu.CompilerParams(
            dimension_semantics=("arbitrary", "arbitrary"), vmem_limit_bytes=VMEM_LIMIT),
        name="fourier",
    )(u, um, g, gt, twc, tws, cs)


def _fourier_constants(n2):
    n = DFT_OUTER * n2
    b = jnp.arange(n2, dtype=jnp.int32)
    d = jnp.arange(1, n2, dtype=jnp.int32)
    th = ((d[:, None] * b[None, :]) % n2).astype(F32) * (2.0 * math.pi / n2)
    full = jnp.concatenate([jnp.cos(th), jnp.sin(th)], axis=0) * (1.0 / math.sqrt(n))
    g, gt = full[:, :n2 - 1].astype(BF16), full[:, n2 - 1:]
    c = jnp.arange(DFT_OUTER, dtype=jnp.int32)
    tw = ((b[:, None] * c[None, :]) % n).astype(F32) * (2.0 * math.pi / n)
    return g, gt, jnp.cos(tw), jnp.sin(tw)


def _channel_dft_matrix():
    ch = jnp.arange(FN_GROUP_DIM, dtype=jnp.int32)
    th = ((ch[:, None] * ch[None, :]) % FN_GROUP_DIM).astype(F32) * (2.0 * math.pi / FN_GROUP_DIM)
    scale = 1.0 / math.sqrt(FN_GROUP_DIM)
    return jnp.concatenate([jnp.cos(th) * scale, jnp.sin(th) * scale], axis=0).astype(BF16)


def _trunk_body(x_ref, o_ref, f_ref, gmix_ref, wg_ref, wna_ref, wfn_ref, wout_ref, gmlp_ref,
                wup_ref, wdn_ref, gfin_ref, y_ref, *, ff_chunk, n_sub):
    def dot(a, b):
        return jnp.dot(a, b, preferred_element_type=F32)

    sub = x_ref.shape[1] // n_sub
    n_ff = D_FF // ff_chunk

    def chain(r):
        rows = slice(r * sub, (r + 1) * sub)
        x = x_ref[0, rows, :]
        h = _rms(x, gmix_ref[...]).astype(BF16)
        gates = dot(h, wg_ref[...])
        y_na = dot(o_ref[0, rows, :], wna_ref[...])
        y_fn = dot(f_ref[0, rows, :].astype(BF16), wfn_ref[...])
        yield
        mixed = jax.nn.sigmoid(gates[:, :D_MODEL]) * y_na + jax.nn.sigmoid(gates[:, D_MODEL:]) * y_fn
        x1 = x + dot(mixed.astype(BF16), wout_ref[...])
        yield
        h2 = _rms(x1, gmlp_ref[...]).astype(BF16)
        acc = x1
        up = dot(h2, wup_ref[:, 0:ff_chunk])
        yield
        for c in range(n_ff):
            a = jnp.maximum(up, 0.0)
            a = (a * a).astype(BF16)
            if c + 1 < n_ff:
                up = dot(h2, wup_ref[:, (c + 1) * ff_chunk:(c + 2) * ff_chunk])
            acc = acc + dot(a, wdn_ref[c * ff_chunk:(c + 1) * ff_chunk, :])
            yield
        y_ref[0, rows, :] = _rms(acc, gfin_ref[...])

    live = [chain(r) for r in range(n_sub)]
    while live:
        for g in list(live):
            try:
                next(g)
            except StopIteration:
                live.remove(g)


def _trunk(x, o, f, gmix, wg, wna, wfn, wout, gmlp, wup, wdn, gfin, tm):
    B, T, D = x.shape
    body = functools.partial(_trunk_body, ff_chunk=1024, n_sub=tm // ROW_BLOCK)

    def const(arr):
        return pl.BlockSpec(arr.shape, lambda b, i: (0, 0), pipeline_mode=pl.Buffered(1))

    return pl.pallas_call(
        body,
        grid=(B, T // tm),
        in_specs=[
            pl.BlockSpec((1, tm, D), lambda b, i: (b, i, 0)),
            pl.BlockSpec((1, tm, NA_WIDTH), lambda b, i: (b, i, 0)),
            pl.BlockSpec((1, tm, FN_WIDTH), lambda b, i: (b, i, 0)),
            const(gmix), const(wg), const(wna), const(wfn), const(wout), const(gmlp),
            const(wup), const(wdn), const(gfin),
        ],
        out_specs=pl.BlockSpec((1, tm, D), lambda b, i: (b, i, 0)),
        out_shape=jax.ShapeDtypeStruct((B, T, D), F32),
        compiler_params=pltpu.CompilerParams(
            dimension_semantics=("arbitrary", "arbitrary"), vmem_limit_bytes=VMEM_LIMIT),
        name="trunk",
    )(x, o, f, gmix, wg, wna, wfn, wout, gmlp, wup, wdn, gfin)


def _pad_rows(a, rows):
    return jnp.concatenate([a, jnp.zeros((rows - a.shape[0], a.shape[1]), a.dtype)], axis=0)


def _run_group(x, meta, consts):
    (w_qkvu, cs, bias, g_mix, wg, wna, wfn, wout, g_mlp, wup, wdn, g_fin) = consts
    B, T, _ = x.shape
    n2 = (N_META + T) // DFT_OUTER
    q, k, v, u = _inproj(x, g_mix, w_qkvu, tm=1024)
    _, km, vm, um = meta
    o = _attention(q, k, v, _pad_rows(km, GRID_W), _pad_rows(vm, GRID_W), bias, rows_per_step=64)
    f = _fourier(u, um, *_fourier_constants(n2), cs)
    return _trunk(x, o, f, g_mix, wg, wna, wfn, wout, g_mlp, wup, wdn, g_fin, tm=512)


def kernel(x_prompt, x_sample, meta_tokens, w_in, rel_bias, meta_bias, w_branch_na, w_branch_fn,
           w_out, g_mix, g_mlp, w_up, w_down, g_final):
    assert w_in.shape[0] == 1, "single-layer trunk"
    w_in0 = w_in[0]
    split = 3 * NA_WIDTH + FN_WIDTH
    w_qkvu = w_in0[:, :split].astype(BF16)
    wg = w_in0[:, split:].astype(BF16)
    cs = _channel_dft_matrix()
    bias = _bias_tables(rel_bias[0], meta_bias[0])
    g_mix2 = g_mix[0][None]
    consts = (w_qkvu, cs, bias, g_mix2, wg, w_branch_na[0].astype(BF16), w_branch_fn[0].astype(BF16),
              w_out[0].astype(BF16), g_mlp[0][None], w_up[0].astype(BF16), w_down[0].astype(BF16),
              g_final[None])
    meta = _inproj(meta_tokens[None], g_mix2, w_qkvu, tm=N_META)
    meta = tuple(m[0] for m in meta)
    return (_run_group(x_prompt, meta, consts), _run_group(x_sample, meta, consts))
```

```python
import functools
import math

import jax
import jax.numpy as jnp
import numpy as np
from jax import lax
from jax.experimental import pallas as pl
from jax.experimental.pallas import tpu as pltpu

F32 = jnp.float32
BF16 = jnp.bfloat16

D_MODEL = 1024
N_META = 16
GRID_W = 64
NA_HEADS = 8
NA_HEAD_DIM = 64
NA_WIDTH = NA_HEADS * NA_HEAD_DIM
NA_WIN_ROWS = 8
NA_WIN_COLS = 16
FN_GROUPS = 4
FN_GROUP_DIM = 128
FN_WIDTH = FN_GROUPS * FN_GROUP_DIM
D_FF = 4 * D_MODEL
RMS_EPS = 1e-6

LANES = 128
SUBLANES = 8
HEAD_PAIRS = NA_HEADS // 2
WIN_SLOTS = NA_WIN_ROWS + 2
WIN_KEYS = WIN_SLOTS * GRID_W
LOCAL_KEYS = (NA_WIN_ROWS + 1) * GRID_W
SCORES_AHEAD = 2
ROW_BLOCK = 256
NEG = -1e30
LOG2E = math.log2(math.e)
DFT_OUTER = 16
VMEM_LIMIT = 56 * 1024 * 1024
TRUNK_VMEM_LIMIT = 60 * 1024 * 1024


def _rms(x, g):
    ms = jnp.mean(x * x, axis=-1, keepdims=True)
    return x * lax.rsqrt(ms + RMS_EPS) * g


def _inproj_body(x_ref, g_ref, w_ref, q_ref, k_ref, v_ref, u_ref, *, n_sub):
    sub = x_ref.shape[1] // n_sub
    for r in range(n_sub):
        rows = slice(r * sub, (r + 1) * sub)
        h = _rms(x_ref[0, rows, :], g_ref[...]).astype(BF16)
        z = jnp.dot(h, w_ref[...], preferred_element_type=F32)
        q_ref[0, rows, :] = (z[:, :NA_WIDTH] * (NA_HEAD_DIM ** -0.5 * LOG2E)).astype(BF16)
        k_ref[0, rows, :] = z[:, NA_WIDTH:2 * NA_WIDTH].astype(BF16)
        v_ref[0, rows, :] = z[:, 2 * NA_WIDTH:3 * NA_WIDTH].astype(BF16)
        u_ref[0, rows, :] = z[:, 3 * NA_WIDTH:]


def _inproj(x, g, w, tm):
    B, T, D = x.shape
    tok = pl.BlockSpec((1, tm, NA_WIDTH), lambda b, i: (b, i, 0))
    return pl.pallas_call(
        functools.partial(_inproj_body, n_sub=max(tm // ROW_BLOCK, 1)),
        grid=(B, T // tm),
        in_specs=[
            pl.BlockSpec((1, tm, D), lambda b, i: (b, i, 0)),
            pl.BlockSpec((1, D), lambda b, i: (0, 0)),
            pl.BlockSpec(w.shape, lambda b, i: (0, 0)),
        ],
        out_specs=[tok] * 4,
        out_shape=[jax.ShapeDtypeStruct((B, T, NA_WIDTH), BF16)] * 3
        + [jax.ShapeDtypeStruct((B, T, FN_WIDTH), F32)],
        compiler_params=pltpu.CompilerParams(
            dimension_semantics=("arbitrary", "arbitrary"), vmem_limit_bytes=VMEM_LIMIT),
        name="inproj",
    )(x, g, w)


def _attn_body(q_ref, k_ref, v_ref, km_ref, vm_ref, bias_ref, o_ref, *, rows, rows_per_step):
    chunk = pl.program_id(2)
    first_head = lax.broadcasted_iota(jnp.int32, (2 * GRID_W, LANES), 1) < NA_HEAD_DIM
    first_head_kv = lax.broadcasted_iota(jnp.int32, (WIN_KEYS, LANES), 1) < NA_HEAD_DIM
    km = km_ref[...]
    vm = vm_ref[...]
    zero = jnp.zeros((2 * GRID_W, LANES), BF16)
    ones = jnp.ones((WIN_KEYS, LANES), BF16)

    def scores(i):
        r = chunk * rows_per_step + 2 * i
        start = jnp.clip(r - NA_WIN_ROWS // 2, 0, rows - (NA_WIN_ROWS + 1))
        cfg = jnp.where(r == 0, 0, jnp.where(r == 2, 1, jnp.where(
            r == rows - 4, 3, jnp.where(r == rows - 2, 4, 2))))
        q2 = q_ref[0, i * 2 * GRID_W:(i + 1) * 2 * GRID_W, :]
        qq = jnp.concatenate([jnp.where(first_head, q2, zero), jnp.where(first_head, zero, q2)], axis=0)
        koff = pl.multiple_of(start * GRID_W, GRID_W)
        kw = jnp.concatenate([k_ref[0, pl.ds(koff, LOCAL_KEYS), :], km], axis=0)
        s = lax.dot_general(qq, kw, (((1,), (1,)), ((), ())), preferred_element_type=F32)
        bias = jnp.concatenate(
            [jnp.concatenate([bias_ref[cfg, row, sp, 0, head] for sp in range(WIN_SLOTS // 2)], axis=1)
             for head in range(2) for row in range(2)], axis=0)
        return s + bias, koff

    def finish(i, s, koff):
        vw = jnp.concatenate([v_ref[0, pl.ds(koff, LOCAL_KEYS), :], vm], axis=0)
        v0 = jnp.where(first_head_kv, vw, ones)
        v1 = jnp.where(first_head_kv, ones, vw)
        m = jnp.max(s, axis=-1, keepdims=True)
        p = jnp.exp2(s - m).astype(BF16)
        o0 = jnp.dot(p[:2 * GRID_W], v0, preferred_element_type=F32)
        o1 = jnp.dot(p[2 * GRID_W:], v1, preferred_element_type=F32)
        r0 = o0 / pltpu.roll(o0, NA_HEAD_DIM, axis=1)
        r1 = o1 / pltpu.roll(o1, NA_HEAD_DIM, axis=1)
        o_ref[0, i * 2 * GRID_W:(i + 1) * 2 * GRID_W, :] = jnp.where(first_head, r0, r1).astype(BF16)

    n_pairs = rows_per_step // 2
    queue = [scores(i) for i in range(min(SCORES_AHEAD, n_pairs))]
    for i in range(n_pairs):
        if i + SCORES_AHEAD < n_pairs:
            queue.append(scores(i + SCORES_AHEAD))
        finish(i, *queue.pop(0))


def _attention(q, k, v, km, vm, bias, rows_per_step):
    B, T, _ = q.shape
    rows = T // GRID_W
    tq = rows_per_step * GRID_W
    body = functools.partial(_attn_body, rows=rows, rows_per_step=rows_per_step)
    return pl.pallas_call(
        body,
        grid=(B, HEAD_PAIRS, rows // rows_per_step),
        in_specs=[
            pl.BlockSpec((1, tq, LANES), lambda b, h, c: (b, c, h)),
            pl.BlockSpec((1, T, LANES), lambda b, h, c: (b, 0, h)),
            pl.BlockSpec((1, T, LANES), lambda b, h, c: (b, 0, h)),
            pl.BlockSpec((GRID_W, LANES), lambda b, h, c: (0, h)),
            pl.BlockSpec((GRID_W, LANES), lambda b, h, c: (0, h)),
            pl.BlockSpec(bias.shape[:3] + (1,) + bias.shape[4:], lambda b, h, c: (0, 0, 0, h, 0, 0, 0)),
        ],
        out_specs=pl.BlockSpec((1, tq, LANES), lambda b, h, c: (b, c, h)),
        out_shape=jax.ShapeDtypeStruct((B, T, NA_WIDTH), BF16),
        compiler_params=pltpu.CompilerParams(
            dimension_semantics=("arbitrary", "arbitrary", "arbitrary"), vmem_limit_bytes=VMEM_LIMIT),
        name="attention",
    )(q, k, v, km, vm, bias)


PAIR_CONFIGS = (((0, 1), ((0, 8), (0, 8))), ((2, 3), ((0, 8), (0, 8))), ((4, 5), ((0, 8), (1, 9))),
                ((5, 6), ((1, 9), (1, 9))), ((7, 8), ((1, 9), (1, 9))))


def _bias_tables(rel_bias, meta_bias):
    c = np.arange(GRID_W)[:, None]
    j = np.arange(GRID_W)[None, :]
    cs = np.clip(c - NA_WIN_COLS // 2, 0, GRID_W - NA_WIN_COLS)
    valid = (j >= cs) & (j < cs + NA_WIN_COLS)
    n_off = 2 * NA_WIN_COLS - 1
    n_row_off = 2 * NA_WIN_ROWS - 1
    onehot = (valid[None] & ((j - c + NA_WIN_COLS - 1)[None] == np.arange(n_off)[:, None, None])).astype(np.float32)
    hi = lax.Precision.HIGHEST
    toe = jnp.einsum("hrx,xcj->rhcj", rel_bias * LOG2E, onehot, precision=hi)
    toe = toe + np.where(valid, 0.0, NEG).astype(np.float32)
    negblk = jnp.full((1, NA_HEADS, GRID_W, GRID_W), NEG, F32)
    metablk = jnp.concatenate(
        [jnp.broadcast_to(meta_bias[:, None, :] * LOG2E, (NA_HEADS, GRID_W, N_META)),
         jnp.full((NA_HEADS, GRID_W, GRID_W - N_META), NEG, F32)], axis=-1)[None]
    blocks = jnp.concatenate([toe, negblk, metablk], axis=0)
    source = np.zeros((len(PAIR_CONFIGS), 2, WIN_SLOTS), np.int32)
    for kk, (qslots, wins) in enumerate(PAIR_CONFIGS):
        for i, (a, (w0, w1)) in enumerate(zip(qslots, wins)):
            for w in range(WIN_SLOTS - 1):
                source[kk, i, w] = w - a + NA_WIN_ROWS - 1 if w0 <= w < w1 else n_row_off
            source[kk, i, WIN_SLOTS - 1] = n_row_off + 1
    zeros = jnp.zeros_like(blocks)
    left = jnp.concatenate([blocks, zeros], axis=-1)
    right = jnp.concatenate([zeros, blocks], axis=-1)

    def assemble(left_ref, right_ref, out_ref):
        for kk in range(source.shape[0]):
            for i in range(2):
                for sp in range(WIN_SLOTS // 2):
                    out_ref[kk, i, sp] = (left_ref[int(source[kk, i, 2 * sp])]
                                          + right_ref[int(source[kk, i, 2 * sp + 1])])

    t = pl.pallas_call(
        assemble,
        out_shape=jax.ShapeDtypeStruct(source.shape[:2] + (WIN_SLOTS // 2, NA_HEADS, GRID_W, 2 * GRID_W), F32),
        compiler_params=pltpu.CompilerParams(vmem_limit_bytes=VMEM_LIMIT),
        name="bias_assemble",
    )(left, right)
    return t.reshape(t.shape[:3] + (HEAD_PAIRS, 2, GRID_W, 2 * GRID_W))


def _snap(x):
    for v in (0.0, 1.0, -1.0):
        if abs(x - v) < 1e-12:
            return v
    return x


def _scaled(x, coef):
    if x is None or coef == 0.0:
        return None
    if coef == 1.0:
        return x
    if coef == -1.0:
        return -x
    return x * coef


def _plus(a, b):
    if a is None:
        return b
    if b is None:
        return a
    return a + b


def _minus(a, b):
    if b is None:
        return a
    if a is None:
        return -b
    return a - b


def _fft_dit(zs, keep=None):
    n = len(zs)
    if n == 1:
        return zs
    even, odd = _fft_dit(zs[0::2]), _fft_dit(zs[1::2])
    out = [None] * n
    for kk in range(n // 2):
        c, s = _snap(math.cos(2 * math.pi * kk / n)), _snap(math.sin(2 * math.pi * kk / n))
        orr, oi = odd[kk]
        tr = _plus(_scaled(orr, c), _scaled(oi, s))
        ti = _plus(_scaled(oi, c), _scaled(orr, -s))
        er, ei = even[kk]
        if keep is None or kk in keep:
            out[kk] = (_plus(er, tr), _plus(ei, ti))
        if keep is None or kk + n // 2 in keep:
            out[kk + n // 2] = (_minus(er, tr), _minus(ei, ti))
    return out


def _dft16_real(xs):
    half = DFT_OUTER // 2
    return _fft_dit([(x, None) for x in xs], keep=set(range(half + 1)))[:half + 1]


def _fourier_body(u_ref, um_ref, g_ref, gt_ref, twc_ref, tws_ref, cs_ref, f_ref,
                  zr_ref, zi_ref, sh_ref, *, n2):
    nm = n2 - 1
    half = DFT_OUTER // 2
    def stage1(r0, meta_rows):
        xs = [um_ref[pl.ds(r0, SUBLANES), :] if (a == 0 and meta_rows)
              else u_ref[0, pl.ds(n2 * a - N_META + r0, SUBLANES), :] for a in range(DFT_OUTER)]
        for c, (zr, zi) in enumerate(_dft16_real(xs)):
            zr_ref[c, pl.ds(r0, SUBLANES), :] = zr
            if zi is not None:
                zi_ref[c, pl.ds(r0, SUBLANES), :] = zi

    for i in range(N_META // SUBLANES):
        stage1(i * SUBLANES, True)

    def stage1_step(i, carry):
        stage1(pl.multiple_of(i * SUBLANES, SUBLANES), False)
        return carry

    lax.fori_loop(N_META // SUBLANES, nm // SUBLANES, stage1_step, 0, unroll=2)
    zt = _dft16_real([u_ref[0, pl.ds(n2 * a + nm - N_META, 1), :] for a in range(DFT_OUTER)])

    g = g_ref[...]
    gt = gt_ref[...]
    csm = cs_ref[...]
    inv_sqrt_n = 1.0 / math.sqrt(DFT_OUTER * n2)

    def prep(s):
        zr = zr_ref[s]
        ztr, zti = zt[s]
        zi = None if s in (0, half) else zi_ref[s]
        if s > 0:
            tc, ts = twc_ref[0:nm, s:s + 1], tws_ref[0:nm, s:s + 1]
            tct, tst = twc_ref[nm:n2, s:s + 1], tws_ref[nm:n2, s:s + 1]
            if zi is None:
                zr, zi = zr * tc, -(zr * ts)
                ztr, zti = ztr * tct, -(ztr * tst)
            else:
                zr, zi = zr * tc + zi * ts, zi * tc - zr * ts
                ztr, zti = ztr * tct + zti * tst, zti * tct - ztr * tst
        if zi is None:
            zi, zti = jnp.zeros_like(zr), jnp.zeros_like(ztr)
        z = jnp.concatenate([zr, zi], axis=1)
        tail = jnp.concatenate([ztr, zti], axis=1)
        return z, tail

    def dense(z, tail):
        p = jnp.dot(g, z.astype(BF16), preferred_element_type=F32)
        return p + gt * tail

    def channel_store(c, pr, pi):
        y = jnp.dot(jnp.concatenate([pr, pi], axis=1).astype(BF16), csm, preferred_element_type=F32)
        f_ref[0, pl.ds(c, nm, stride=DFT_OUTER), :] = y

    def post(s, p, z, tail):
        czr, czi, szr, szi = p[:nm, :LANES], p[:nm, LANES:], p[nm:, :LANES], p[nm:, LANES:]
        channel_store(s, czr + szi, czi - szr)
        if 0 < s < half:
            col = (jnp.sum(z, axis=0, keepdims=True) + tail) * inv_sqrt_n
            sh_ref[0, 0:nm] = czr - szi
            sh_ref[1, 0:nm] = -(czi + szr)
            sh_ref[0, nm:n2] = col[:, :LANES]
            sh_ref[1, nm:n2] = -col[:, LANES:]
            channel_store(DFT_OUTER - s, sh_ref[0, 1:n2], sh_ref[1, 1:n2])

    pending = None
    for s in range(half + 1):
        z, tail = prep(s)
        p = dense(z, tail)
        if pending is not None:
            post(*pending)
        pending = (s, p, z, tail)
    post(*pending)


def _fourier(u, um, g, gt, twc, tws, cs):
    B, T, _ = u.shape
    n2 = (N_META + T) // DFT_OUTER
    nm = n2 - 1
    body = functools.partial(_fourier_body, n2=n2)

    def const(arr):
        return pl.BlockSpec(arr.shape, lambda b, j: (0, 0))

    return pl.pallas_call(
        body,
        grid=(B, FN_GROUPS),
        in_specs=[
            pl.BlockSpec((1, T, LANES), lambda b, j: (b, 0, j)),
            pl.BlockSpec((N_META, LANES), lambda b, j: (0, j)),
            const(g), const(gt), const(twc), const(tws), const(cs),
        ],
        out_specs=pl.BlockSpec((1, T, LANES), lambda b, j: (b, 0, j)),
        out_shape=jax.ShapeDtypeStruct((B, T, FN_WIDTH), F32),
        scratch_shapes=[
            pltpu.VMEM((DFT_OUTER // 2 + 1, nm, LANES), F32),
            pltpu.VMEM((DFT_OUTER // 2 + 1, nm, LANES), F32),
            pltpu.VMEM((2, nm + SUBLANES, LANES), F32),
        ],
        compiler_params=pltpu.CompilerParams(
            dimension_semantics=("arbitrary", "arbitrary"), vmem_limit_bytes=VMEM_LIMIT),
        name="fourier",
    )(u, um, g, gt, twc, tws, cs)


def _fourier_constants(n2):
    n = DFT_OUTER * n2
    b = jnp.arange(n2, dtype=jnp.int32)
    d = jnp.arange(1, n2, dtype=jnp.int32)
    th = ((d[:, None] * b[None, :]) % n2).astype(F32) * (2.0 * math.pi / n2)
    full = jnp.concatenate([jnp.cos(th), jnp.sin(th)], axis=0) * (1.0 / math.sqrt(n))
    g, gt = full[:, :n2 - 1].astype(BF16), full[:, n2 - 1:]
    c = jnp.arange(DFT_OUTER, dtype=jnp.int32)
    tw = ((b[:, None] * c[None, :]) % n).astype(F32) * (2.0 * math.pi / n)
    return g, gt, jnp.cos(tw), jnp.sin(tw)


def _channel_dft_matrix():
    ch = jnp.arange(FN_GROUP_DIM, dtype=jnp.int32)
    th = ((ch[:, None] * ch[None, :]) % FN_GROUP_DIM).astype(F32) * (2.0 * math.pi / FN_GROUP_DIM)
    scale = 1.0 / math.sqrt(FN_GROUP_DIM)
    return jnp.concatenate([jnp.cos(th) * scale, jnp.sin(th) * scale], axis=0).astype(BF16)


def _trunk_body(x_ref, o_ref, f_ref, gmix_ref, wg_ref, wna_ref, wfn_ref, wout_ref, gmlp_ref,
                wup_ref, wdn_ref, gfin_ref, y_ref, *, ff_chunk, n_sub):
    def dot(a, b):
        return jnp.dot(a, b, preferred_element_type=F32)

    sub = x_ref.shape[1] // n_sub
    n_ff = D_FF // ff_chunk

    def chain(r):
        rows = slice(r * sub, (r + 1) * sub)
        x = x_ref[0, rows, :]
        h = _rms(x, gmix_ref[...]).astype(BF16)
        gates = dot(h, wg_ref[...])
        y_na = dot(o_ref[0, rows, :], wna_ref[...])
        y_fn = dot(f_ref[0, rows, :].astype(BF16), wfn_ref[...])
        yield
        mixed = jax.nn.sigmoid(gates[:, :D_MODEL]) * y_na + jax.nn.sigmoid(gates[:, D_MODEL:]) * y_fn
        x1 = x + dot(mixed.astype(BF16), wout_ref[...])
        yield
        h2 = _rms(x1, gmlp_ref[...]).astype(BF16)
        acc = x1
        up = dot(h2, wup_ref[:, 0:ff_chunk])
        yield
        for c in range(n_ff):
            a = jnp.maximum(up, 0.0)
            a = (a * a).astype(BF16)
            if c + 1 < n_ff:
                up = dot(h2, wup_ref[:, (c + 1) * ff_chunk:(c + 2) * ff_chunk])
            acc = acc + dot(a, wdn_ref[c * ff_chunk:(c + 1) * ff_chunk, :])
            yield
        y_ref[0, rows, :] = _rms(acc, gfin_ref[...])

    live = [chain(r) for r in range(n_sub)]
    while live:
        for g in list(live):
            try:
                next(g)
            except StopIteration:
                live.remove(g)


def _trunk(x, o, f, gmix, wg, wna, wfn, wout, gmlp, wup, wdn, gfin, tm):
    B, T, D = x.shape
    body = functools.partial(_trunk_body, ff_chunk=512, n_sub=tm // ROW_BLOCK)

    def const(arr):
        return pl.BlockSpec(arr.shape, lambda b, i: (0, 0), pipeline_mode=pl.Buffered(1))

    return pl.pallas_call(
        body,
        grid=(B, T // tm),
        in_specs=[
            pl.BlockSpec((1, tm, D), lambda b, i: (b, i, 0)),
            pl.BlockSpec((1, tm, NA_WIDTH), lambda b, i: (b, i, 0)),
            pl.BlockSpec((1, tm, FN_WIDTH), lambda b, i: (b, i, 0)),
            const(gmix), const(wg), const(wna), const(wfn), const(wout), const(gmlp),
            const(wup), const(wdn), const(gfin),
        ],
        out_specs=pl.BlockSpec((1, tm, D), lambda b, i: (b, i, 0)),
        out_shape=jax.ShapeDtypeStruct((B, T, D), F32),
        compiler_params=pltpu.CompilerParams(
            dimension_semantics=("arbitrary", "arbitrary"), vmem_limit_bytes=TRUNK_VMEM_LIMIT),
        name="trunk",
    )(x, o, f, gmix, wg, wna, wfn, wout, gmlp, wup, wdn, gfin)


def _pad_rows(a, rows):
    return jnp.concatenate([a, jnp.zeros((rows - a.shape[0], a.shape[1]), a.dtype)], axis=0)


def _run_group(x, meta, consts):
    (w_qkvu, cs, bias, g_mix, wg, wna, wfn, wout, g_mlp, wup, wdn, g_fin) = consts
    B, T, _ = x.shape
    n2 = (N_META + T) // DFT_OUTER
    q, k, v, u = _inproj(x, g_mix, w_qkvu, tm=1024)
    _, km, vm, um = meta
    o = _attention(q, k, v, _pad_rows(km, GRID_W), _pad_rows(vm, GRID_W), bias, rows_per_step=64)
    f = _fourier(u, um, *_fourier_constants(n2), cs)
    return _trunk(x, o, f, g_mix, wg, wna, wfn, wout, g_mlp, wup, wdn, g_fin, tm=1024)


def kernel(x_prompt, x_sample, meta_tokens, w_in, rel_bias, meta_bias, w_branch_na, w_branch_fn,
           w_out, g_mix, g_mlp, w_up, w_down, g_final):
    assert w_in.shape[0] == 1, "single-layer trunk"
    w_in0 = w_in[0]
    split = 3 * NA_WIDTH + FN_WIDTH
    w_qkvu = w_in0[:, :split].astype(BF16)
    wg = w_in0[:, split:].astype(BF16)
    cs = _channel_dft_matrix()
    bias = _bias_tables(rel_bias[0], meta_bias[0])
    g_mix2 = g_mix[0][None]
    consts = (w_qkvu, cs, bias, g_mix2, wg, w_branch_na[0].astype(BF16), w_branch_fn[0].astype(BF16),
              w_out[0].astype(BF16), g_mlp[0][None], w_up[0].astype(BF16), w_down[0].astype(BF16),
              g_final[None])
    meta = _inproj(meta_tokens[None], g_mix2, w_qkvu, tm=N_META)
    meta = tuple(m[0] for m in meta)
    return (_run_group(x_prompt, meta, consts), _run_group(x_sample, meta, consts))
```

```python
import functools
import math

import jax
import jax.numpy as jnp
import numpy as np
from jax import lax
from jax.experimental import pallas as pl
from jax.experimental.pallas import tpu as pltpu

F32 = jnp.float32
BF16 = jnp.bfloat16

D_MODEL = 1024
N_META = 16
GRID_W = 64
NA_HEADS = 8
NA_HEAD_DIM = 64
NA_WIDTH = NA_HEADS * NA_HEAD_DIM
NA_WIN_ROWS = 8
NA_WIN_COLS = 16
FN_GROUPS = 4
FN_GROUP_DIM = 128
FN_WIDTH = FN_GROUPS * FN_GROUP_DIM
D_FF = 4 * D_MODEL
RMS_EPS = 1e-6

LANES = 128
SUBLANES = 8
HEAD_PAIRS = NA_HEADS // 2
WIN_SLOTS = NA_WIN_ROWS + 2
WIN_KEYS = WIN_SLOTS * GRID_W
LOCAL_KEYS = (NA_WIN_ROWS + 1) * GRID_W
SCORES_AHEAD = 2
ROW_BLOCK = 256
NEG = -1e30
LOG2E = math.log2(math.e)
DFT_OUTER = 16
VMEM_LIMIT = 56 * 1024 * 1024
INPROJ_TOKENS = 4 * ROW_BLOCK
TRUNK_TOKENS = 2 * ROW_BLOCK
ATTN_ROWS_PER_STEP = 64
FF_CHUNK = 1024


def _rms(x, g):
    ms = jnp.mean(x * x, axis=-1, keepdims=True)
    return x * lax.rsqrt(ms + RMS_EPS) * g


def _inproj_body(x_ref, g_ref, w_ref, q_ref, k_ref, v_ref, u_ref, *, n_sub):
    sub = x_ref.shape[1] // n_sub
    for r in range(n_sub):
        rows = slice(r * sub, (r + 1) * sub)
        h = _rms(x_ref[0, rows, :], g_ref[...]).astype(BF16)
        z = jnp.dot(h, w_ref[...], preferred_element_type=F32)
        q_ref[0, rows, :] = (z[:, :NA_WIDTH] * (NA_HEAD_DIM ** -0.5 * LOG2E)).astype(BF16)
        k_ref[0, rows, :] = z[:, NA_WIDTH:2 * NA_WIDTH].astype(BF16)
        v_ref[0, rows, :] = z[:, 2 * NA_WIDTH:3 * NA_WIDTH].astype(BF16)
        u_ref[0, rows, :] = z[:, 3 * NA_WIDTH:]


def _inproj(x, g, w, tm):
    B, T, D = x.shape
    assert T % tm == 0 and (tm % ROW_BLOCK == 0 or tm < ROW_BLOCK)
    tok = pl.BlockSpec((1, tm, NA_WIDTH), lambda b, i: (b, i, 0))
    return pl.pallas_call(
        functools.partial(_inproj_body, n_sub=max(tm // ROW_BLOCK, 1)),
        grid=(B, T // tm),
        in_specs=[
            pl.BlockSpec((1, tm, D), lambda b, i: (b, i, 0)),
            pl.BlockSpec((1, D), lambda b, i: (0, 0)),
            pl.BlockSpec(w.shape, lambda b, i: (0, 0)),
        ],
        out_specs=[tok] * 4,
        out_shape=[jax.ShapeDtypeStruct((B, T, NA_WIDTH), BF16)] * 3
        + [jax.ShapeDtypeStruct((B, T, FN_WIDTH), F32)],
        compiler_params=pltpu.CompilerParams(
            dimension_semantics=("arbitrary", "arbitrary"), vmem_limit_bytes=VMEM_LIMIT),
        name="inproj",
    )(x, g, w)


def _attn_body(q_ref, k_ref, v_ref, km_ref, vm_ref, bias_ref, o_ref, *, rows, rows_per_step):
    chunk = pl.program_id(2)
    first_head = lax.broadcasted_iota(jnp.int32, (2 * GRID_W, LANES), 1) < NA_HEAD_DIM
    first_head_kv = lax.broadcasted_iota(jnp.int32, (WIN_KEYS, LANES), 1) < NA_HEAD_DIM
    km = km_ref[...]
    vm = vm_ref[...]
    zero = jnp.zeros((2 * GRID_W, LANES), BF16)
    ones = jnp.ones((WIN_KEYS, LANES), BF16)

    def scores(i):
        r = chunk * rows_per_step + 2 * i
        start = jnp.clip(r - NA_WIN_ROWS // 2, 0, rows - (NA_WIN_ROWS + 1))
        cfg = jnp.where(r == 0, 0, jnp.where(r == 2, 1, jnp.where(
            r == rows - 4, 3, jnp.where(r == rows - 2, 4, 2))))
        q2 = q_ref[0, i * 2 * GRID_W:(i + 1) * 2 * GRID_W, :]
        qq = jnp.concatenate([jnp.where(first_head, q2, zero), jnp.where(first_head, zero, q2)], axis=0)
        koff = pl.multiple_of(start * GRID_W, GRID_W)
        kw = jnp.concatenate([k_ref[0, pl.ds(koff, LOCAL_KEYS), :], km], axis=0)
        s = lax.dot_general(qq, kw, (((1,), (1,)), ((), ())), preferred_element_type=F32)
        bias = jnp.concatenate(
            [jnp.concatenate([bias_ref[cfg, row, sp, 0, head] for sp in range(WIN_SLOTS // 2)], axis=1)
             for head in range(2) for row in range(2)], axis=0)
        return s + bias, koff

    def finish(i, s, koff):
        vw = jnp.concatenate([v_ref[0, pl.ds(koff, LOCAL_KEYS), :], vm], axis=0)
        v0 = jnp.where(first_head_kv, vw, ones)
        v1 = jnp.where(first_head_kv, ones, vw)
        m = jnp.max(s, axis=-1, keepdims=True)
        p = jnp.exp2(s - m).astype(BF16)
        o0 = jnp.dot(p[:2 * GRID_W], v0, preferred_element_type=F32)
        o1 = jnp.dot(p[2 * GRID_W:], v1, preferred_element_type=F32)
        r0 = o0 / pltpu.roll(o0, NA_HEAD_DIM, axis=1)
        r1 = o1 / pltpu.roll(o1, NA_HEAD_DIM, axis=1)
        o_ref[0, i * 2 * GRID_W:(i + 1) * 2 * GRID_W, :] = jnp.where(first_head, r0, r1).astype(BF16)

    n_pairs = rows_per_step // 2
    queue = [scores(i) for i in range(min(SCORES_AHEAD, n_pairs))]
    for i in range(n_pairs):
        if i + SCORES_AHEAD < n_pairs:
            queue.append(scores(i + SCORES_AHEAD))
        finish(i, *queue.pop(0))


def _attention(q, k, v, km, vm, bias, rows_per_step):
    B, T, _ = q.shape
    rows = T // GRID_W
    assert T % GRID_W == 0 and rows % rows_per_step == 0 and rows >= NA_WIN_ROWS + 2
    tq = rows_per_step * GRID_W
    body = functools.partial(_attn_body, rows=rows, rows_per_step=rows_per_step)
    return pl.pallas_call(
        body,
        grid=(B, HEAD_PAIRS, rows // rows_per_step),
        in_specs=[
            pl.BlockSpec((1, tq, LANES), lambda b, h, c: (b, c, h)),
            pl.BlockSpec((1, T, LANES), lambda b, h, c: (b, 0, h)),
            pl.BlockSpec((1, T, LANES), lambda b, h, c: (b, 0, h)),
            pl.BlockSpec((GRID_W, LANES), lambda b, h, c: (0, h)),
            pl.BlockSpec((GRID_W, LANES), lambda b, h, c: (0, h)),
            pl.BlockSpec(bias.shape[:3] + (1,) + bias.shape[4:], lambda b, h, c: (0, 0, 0, h, 0, 0, 0)),
        ],
        out_specs=pl.BlockSpec((1, tq, LANES), lambda b, h, c: (b, c, h)),
        out_shape=jax.ShapeDtypeStruct((B, T, NA_WIDTH), BF16),
        compiler_params=pltpu.CompilerParams(
            dimension_semantics=("arbitrary", "arbitrary", "arbitrary"), vmem_limit_bytes=VMEM_LIMIT),
        name="attention",
    )(q, k, v, km, vm, bias)


PAIR_CONFIGS = (((0, 1), ((0, 8), (0, 8))), ((2, 3), ((0, 8), (0, 8))), ((4, 5), ((0, 8), (1, 9))),
                ((5, 6), ((1, 9), (1, 9))), ((7, 8), ((1, 9), (1, 9))))


def _bias_tables(rel_bias, meta_bias):
    c = np.arange(GRID_W)[:, None]
    j = np.arange(GRID_W)[None, :]
    cs = np.clip(c - NA_WIN_COLS // 2, 0, GRID_W - NA_WIN_COLS)
    valid = (j >= cs) & (j < cs + NA_WIN_COLS)
    n_off = 2 * NA_WIN_COLS - 1
    n_row_off = 2 * NA_WIN_ROWS - 1
    onehot = (valid[None] & ((j - c + NA_WIN_COLS - 1)[None] == np.arange(n_off)[:, None, None])).astype(np.float32)
    hi = lax.Precision.HIGHEST
    toe = jnp.einsum("hrx,xcj->rhcj", rel_bias * LOG2E, onehot, precision=hi)
    toe = toe + np.where(valid, 0.0, NEG).astype(np.float32)
    negblk = jnp.full((1, NA_HEADS, GRID_W, GRID_W), NEG, F32)
    metablk = jnp.concatenate(
        [jnp.broadcast_to(meta_bias[:, None, :] * LOG2E, (NA_HEADS, GRID_W, N_META)),
         jnp.full((NA_HEADS, GRID_W, GRID_W - N_META), NEG, F32)], axis=-1)[None]
    blocks = jnp.concatenate([toe, negblk, metablk], axis=0)
    source = np.zeros((len(PAIR_CONFIGS), 2, WIN_SLOTS), np.int32)
    for kk, (qslots, wins) in enumerate(PAIR_CONFIGS):
        for i, (a, (w0, w1)) in enumerate(zip(qslots, wins)):
            for w in range(WIN_SLOTS - 1):
                source[kk, i, w] = w - a + NA_WIN_ROWS - 1 if w0 <= w < w1 else n_row_off
            source[kk, i, WIN_SLOTS - 1] = n_row_off + 1
    zeros = jnp.zeros_like(blocks)
    left = jnp.concatenate([blocks, zeros], axis=-1)
    right = jnp.concatenate([zeros, blocks], axis=-1)

    def assemble(left_ref, right_ref, out_ref):
        for kk in range(source.shape[0]):
            for i in range(2):
                for sp in range(WIN_SLOTS // 2):
                    out_ref[kk, i, sp] = (left_ref[int(source[kk, i, 2 * sp])]
                                          + right_ref[int(source[kk, i, 2 * sp + 1])])

    t = pl.pallas_call(
        assemble,
        out_shape=jax.ShapeDtypeStruct(source.shape[:2] + (WIN_SLOTS // 2, NA_HEADS, GRID_W, 2 * GRID_W), F32),
        compiler_params=pltpu.CompilerParams(vmem_limit_bytes=VMEM_LIMIT),
        name="bias_assemble",
    )(left, right)
    return t.reshape(t.shape[:3] + (HEAD_PAIRS, 2, GRID_W, 2 * GRID_W))


def _snap(x):
    for v in (0.0, 1.0, -1.0):
        if abs(x - v) < 1e-12:
            return v
    return x


def _scaled(x, coef):
    if x is None or coef == 0.0:
        return None
    arr, sign = x
    coef = coef * sign
    if abs(coef) == 1.0:
        return arr, coef
    return arr * coef, 1.0


def _plus(a, b):
    if a is None:
        return b
    if b is None:
        return a
    (x, sx), (y, sy) = a, b
    if sx == sy:
        return x + y, sx
    return (x - y, 1.0) if sx > 0 else (y - x, 1.0)


def _negated(a):
    return None if a is None else (a[0], -a[1])


def _value(a):
    if a is None:
        return None
    return a[0] if a[1] > 0 else -a[0]


def _fft_real(xs):
    n = len(xs)
    if n == 1:
        return [(xs[0], None)]
    even, odd = _fft_real(xs[0::2]), _fft_real(xs[1::2])

    def bin_of(half, kk):
        if kk <= n // 4:
            return half[kk]
        re, im = half[n // 2 - kk]
        return re, _negated(im)

    out = []
    for kk in range(n // 2 + 1):
        c, s = _snap(math.cos(2 * math.pi * kk / n)), _snap(math.sin(2 * math.pi * kk / n))
        er, ei = bin_of(even, kk % (n // 2))
        orr, oi = bin_of(odd, kk % (n // 2))
        tr = _plus(_scaled(orr, c), _scaled(oi, s))
        ti = _plus(_scaled(oi, c), _scaled(orr, -s))
        out.append((_plus(er, tr), _plus(ei, ti)))
    return out


def _dft16_real(xs):
    return [(_value(re), _value(im)) for re, im in _fft_real([(x, 1.0) for x in xs])]


def _fourier_body(u_ref, um_ref, g_ref, gt_ref, twc_ref, tws_ref, cs_ref, f_ref,
                  zr_ref, zi_ref, sh_ref, *, n2):
    nm = n2 - 1
    half = DFT_OUTER // 2
    def stage1(r0, meta_rows):
        xs = [um_ref[pl.ds(r0, SUBLANES), :] if (a == 0 and meta_rows)
              else u_ref[0, pl.ds(n2 * a - N_META + r0, SUBLANES), :] for a in range(DFT_OUTER)]
        for c, (zr, zi) in enumerate(_dft16_real(xs)):
            zr_ref[c, pl.ds(r0, SUBLANES), :] = zr
            if zi is not None:
                zi_ref[c, pl.ds(r0, SUBLANES), :] = zi

    for i in range(N_META // SUBLANES):
        stage1(i * SUBLANES, True)

    def stage1_step(i, carry):
        stage1(pl.multiple_of(i * SUBLANES, SUBLANES), False)
        return carry

    lax.fori_loop(N_META // SUBLANES, nm // SUBLANES, stage1_step, 0, unroll=2)
    zt = _dft16_real([u_ref[0, pl.ds(n2 * a + nm - N_META, 1), :] for a in range(DFT_OUTER)])

    g = g_ref[...]
    gt = gt_ref[...]
    csm = cs_ref[...]
    inv_sqrt_n = 1.0 / math.sqrt(DFT_OUTER * n2)

    def prep(s):
        zr = zr_ref[s]
        ztr, zti = zt[s]
        zi = None if s in (0, half) else zi_ref[s]
        if s > 0:
            tc, ts = twc_ref[0:nm, s:s + 1], tws_ref[0:nm, s:s + 1]
            tct, tst = twc_ref[nm:n2, s:s + 1], tws_ref[nm:n2, s:s + 1]
            if zi is None:
                zr, zi = zr * tc, -(zr * ts)
                ztr, zti = ztr * tct, -(ztr * tst)
            else:
                zr, zi = zr * tc + zi * ts, zi * tc - zr * ts
                ztr, zti = ztr * tct + zti * tst, zti * tct - ztr * tst
        if zi is None:
            zi, zti = jnp.zeros_like(zr), jnp.zeros_like(ztr)
        z = jnp.concatenate([zr, zi], axis=1)
        tail = jnp.concatenate([ztr, zti], axis=1)
        return z, tail

    def dense(z, tail):
        p = jnp.dot(g, z.astype(BF16), preferred_element_type=F32)
        return p + gt * tail

    def channel_store(c, pr, pi):
        y = jnp.dot(jnp.concatenate([pr, pi], axis=1).astype(BF16), csm, preferred_element_type=F32)
        f_ref[0, pl.ds(c, nm, stride=DFT_OUTER), :] = y

    def post(s, p, z, tail):
        czr, czi, szr, szi = p[:nm, :LANES], p[:nm, LANES:], p[nm:, :LANES], p[nm:, LANES:]
        channel_store(s, czr + szi, czi - szr)
        if 0 < s < half:
            col = (jnp.sum(z, axis=0, keepdims=True) + tail) * inv_sqrt_n
            sh_ref[0, 0:nm] = czr - szi
            sh_ref[1, 0:nm] = -(czi + szr)
            sh_ref[0, nm:n2] = col[:, :LANES]
            sh_ref[1, nm:n2] = -col[:, LANES:]
            channel_store(DFT_OUTER - s, sh_ref[0, 1:n2], sh_ref[1, 1:n2])

    pending = None
    for s in range(half + 1):
        z, tail = prep(s)
        p = dense(z, tail)
        if pending is not None:
            post(*pending)
        pending = (s, p, z, tail)
    post(*pending)


def _fourier(u, um, g, gt, twc, tws, cs):
    B, T, _ = u.shape
    assert N_META == DFT_OUTER and T % (2 * SUBLANES * DFT_OUTER) == 0
    n2 = (N_META + T) // DFT_OUTER
    nm = n2 - 1
    body = functools.partial(_fourier_body, n2=n2)

    def const(arr):
        return pl.BlockSpec(arr.shape, lambda b, j: (0, 0))

    return pl.pallas_call(
        body,
        grid=(B, FN_GROUPS),
        in_specs=[
            pl.BlockSpec((1, T, LANES), lambda b, j: (b, 0, j)),
            pl.BlockSpec((N_META, LANES), lambda b, j: (0, j)),
            const(g), const(gt), const(twc), const(tws), const(cs),
        ],
        out_specs=pl.BlockSpec((1, T, LANES), lambda b, j: (b, 0, j)),
        out_shape=jax.ShapeDtypeStruct((B, T, FN_WIDTH), F32),
        scratch_shapes=[
            pltpu.VMEM((DFT_OUTER // 2 + 1, nm, LANES), F32),
            pltpu.VMEM((DFT_OUTER // 2 + 1, nm, LANES), F32),
            pltpu.VMEM((2, nm + SUBLANES, LANES), F32),
        ],
        compiler_params=pltpu.CompilerParams(
            dimension_semantics=("arbitrary", "arbitrary"), vmem_limit_bytes=VMEM_LIMIT),
        name="fourier",
    )(u, um, g, gt, twc, tws, cs)


def _fourier_constants(n2):
    n = DFT_OUTER * n2
    b = jnp.arange(n2, dtype=jnp.int32)
    d = jnp.arange(1, n2, dtype=jnp.int32)
    th = ((d[:, None] * b[None, :]) % n2).astype(F32) * (2.0 * math.pi / n2)
    full = jnp.concatenate([jnp.cos(th), jnp.sin(th)], axis=0) * (1.0 / math.sqrt(n))
    g, gt = full[:, :n2 - 1].astype(BF16), full[:, n2 - 1:]
    c = jnp.arange(DFT_OUTER, dtype=jnp.int32)
    tw = ((b[:, None] * c[None, :]) % n).astype(F32) * (2.0 * math.pi / n)
    return g, gt, jnp.cos(tw), jnp.sin(tw)


def _channel_dft_matrix():
    ch = jnp.arange(FN_GROUP_DIM, dtype=jnp.int32)
    th = ((ch[:, None] * ch[None, :]) % FN_GROUP_DIM).astype(F32) * (2.0 * math.pi / FN_GROUP_DIM)
    scale = 1.0 / math.sqrt(FN_GROUP_DIM)
    return jnp.concatenate([jnp.cos(th) * scale, jnp.sin(th) * scale], axis=0).astype(BF16)


def _trunk_body(x_ref, o_ref, f_ref, gmix_ref, wg_ref, wna_ref, wfn_ref, wout_ref, gmlp_ref,
                wup_ref, wdn_ref, gfin_ref, y_ref, *, ff_chunk, n_sub):
    def dot(a, b):
        return jnp.dot(a, b, preferred_element_type=F32)

    sub = x_ref.shape[1] // n_sub
    n_ff = D_FF // ff_chunk

    def chain(r):
        rows = slice(r * sub, (r + 1) * sub)
        x = x_ref[0, rows, :]
        h = _rms(x, gmix_ref[...]).astype(BF16)
        gates = dot(h, wg_ref[...])
        y_na = dot(o_ref[0, rows, :], wna_ref[...])
        y_fn = dot(f_ref[0, rows, :].astype(BF16), wfn_ref[...])
        yield
        mixed = jax.nn.sigmoid(gates[:, :D_MODEL]) * y_na + jax.nn.sigmoid(gates[:, D_MODEL:]) * y_fn
        x1 = x + dot(mixed.astype(BF16), wout_ref[...])
        yield
        h2 = _rms(x1, gmlp_ref[...]).astype(BF16)
        acc = x1
        up = dot(h2, wup_ref[:, 0:ff_chunk])
        yield
        for c in range(n_ff):
            a = jnp.maximum(up, 0.0)
            a = (a * a).astype(BF16)
            if c + 1 < n_ff:
                up = dot(h2, wup_ref[:, (c + 1) * ff_chunk:(c + 2) * ff_chunk])
            acc = acc + dot(a, wdn_ref[c * ff_chunk:(c + 1) * ff_chunk, :])
            yield
        y_ref[0, rows, :] = _rms(acc, gfin_ref[...])

    live = [chain(r) for r in range(n_sub)]
    while live:
        for g in list(live):
            try:
                next(g)
            except StopIteration:
                live.remove(g)


def _trunk(x, o, f, gmix, wg, wna, wfn, wout, gmlp, wup, wdn, gfin, tm):
    B, T, D = x.shape
    assert T % tm == 0 and tm % ROW_BLOCK == 0 and D_FF % FF_CHUNK == 0
    body = functools.partial(_trunk_body, ff_chunk=FF_CHUNK, n_sub=tm // ROW_BLOCK)

    def const(arr):
        return pl.BlockSpec(arr.shape, lambda b, i: (0, 0), pipeline_mode=pl.Buffered(1))

    return pl.pallas_call(
        body,
        grid=(B, T // tm),
        in_specs=[
            pl.BlockSpec((1, tm, D), lambda b, i: (b, i, 0)),
            pl.BlockSpec((1, tm, NA_WIDTH), lambda b, i: (b, i, 0)),
            pl.BlockSpec((1, tm, FN_WIDTH), lambda b, i: (b, i, 0)),
            const(gmix), const(wg), const(wna), const(wfn), const(wout), const(gmlp),
            const(wup), const(wdn), const(gfin),
        ],
        out_specs=pl.BlockSpec((1, tm, D), lambda b, i: (b, i, 0)),
        out_shape=jax.ShapeDtypeStruct((B, T, D), F32),
        compiler_params=pltpu.CompilerParams(
            dimension_semantics=("arbitrary", "arbitrary"), vmem_limit_bytes=VMEM_LIMIT),
        name="trunk",
    )(x, o, f, gmix, wg, wna, wfn, wout, gmlp, wup, wdn, gfin)


def _pad_rows(a, rows):
    return jnp.concatenate([a, jnp.zeros((rows - a.shape[0], a.shape[1]), a.dtype)], axis=0)


def _run_group(x, meta, consts):
    (w_qkvu, cs, bias, g_mix, wg, wna, wfn, wout, g_mlp, wup, wdn, g_fin) = consts
    B, T, _ = x.shape
    n2 = (N_META + T) // DFT_OUTER
    q, k, v, u = _inproj(x, g_mix, w_qkvu, tm=INPROJ_TOKENS)
    _, km, vm, um = meta
    o = _attention(q, k, v, _pad_rows(km, GRID_W), _pad_rows(vm, GRID_W), bias, rows_per_step=ATTN_ROWS_PER_STEP)
    f = _fourier(u, um, *_fourier_constants(n2), cs)
    return _trunk(x, o, f, g_mix, wg, wna, wfn, wout, g_mlp, wup, wdn, g_fin, tm=TRUNK_TOKENS)


def kernel(x_prompt, x_sample, meta_tokens, w_in, rel_bias, meta_bias, w_branch_na, w_branch_fn,
           w_out, g_mix, g_mlp, w_up, w_down, g_final):
    assert w_in.shape[0] == 1, "single-layer trunk"
    w_in0 = w_in[0]
    split = 3 * NA_WIDTH + FN_WIDTH
    w_qkvu = w_in0[:, :split].astype(BF16)
    wg = w_in0[:, split:].astype(BF16)
    cs = _channel_dft_matrix()
    bias = _bias_tables(rel_bias[0], meta_bias[0])
    g_mix2 = g_mix[0][None]
    consts = (w_qkvu, cs, bias, g_mix2, wg, w_branch_na[0].astype(BF16), w_branch_fn[0].astype(BF16),
              w_out[0].astype(BF16), g_mlp[0][None], w_up[0].astype(BF16), w_down[0].astype(BF16),
              g_final[None])
    meta = _inproj(meta_tokens[None], g_mix2, w_qkvu, tm=N_META)
    meta = tuple(m[0] for m in meta)
    return (_run_group(x_prompt, meta, consts), _run_group(x_sample, meta, consts))
```

```python
import functools
import math

import jax
import jax.numpy as jnp
import numpy as np
from jax import lax
from jax.experimental import pallas as pl
from jax.experimental.pallas import tpu as pltpu

F32 = jnp.float32
BF16 = jnp.bfloat16

D_MODEL = 1024
N_META = 16
GRID_W = 64
NA_HEADS = 8
NA_HEAD_DIM = 64
NA_WIDTH = NA_HEADS * NA_HEAD_DIM
NA_WIN_ROWS = 8
NA_WIN_COLS = 16
FN_GROUPS = 4
FN_GROUP_DIM = 128
FN_WIDTH = FN_GROUPS * FN_GROUP_DIM
D_FF = 4 * D_MODEL
RMS_EPS = 1e-6

LANES = 128
SUBLANES = 8
HEAD_PAIRS = NA_HEADS // 2
SCORES_AHEAD = 2
ROW_BLOCK = 256
NEG = -1e30
LOG2E = math.log2(math.e)
DFT_OUTER = 16
VMEM_LIMIT = 56 * 1024 * 1024
INPROJ_TOKENS = 4 * ROW_BLOCK
TRUNK_TOKENS = 2 * ROW_BLOCK
ATTN_ROWS_PER_STEP = 64
FF_CHUNK = 1024


def _rms(x, g):
    ms = jnp.mean(x * x, axis=-1, keepdims=True)
    return x * lax.rsqrt(ms + RMS_EPS) * g


def _inproj_body(x_ref, g_ref, w_ref, q_ref, k_ref, v_ref, u_ref, *, n_sub):
    sub = x_ref.shape[1] // n_sub
    for r in range(n_sub):
        rows = slice(r * sub, (r + 1) * sub)
        h = _rms(x_ref[0, rows, :], g_ref[...]).astype(BF16)
        z = jnp.dot(h, w_ref[...], preferred_element_type=F32)
        q_ref[0, rows, :] = (z[:, :NA_WIDTH] * (NA_HEAD_DIM ** -0.5 * LOG2E)).astype(BF16)
        k_ref[0, rows, :] = z[:, NA_WIDTH:2 * NA_WIDTH].astype(BF16)
        v_ref[0, rows, :] = z[:, 2 * NA_WIDTH:3 * NA_WIDTH].astype(BF16)
        u_ref[0, rows, :] = z[:, 3 * NA_WIDTH:]


def _inproj(x, g, w, tm):
    B, T, D = x.shape
    assert T % tm == 0 and (tm % ROW_BLOCK == 0 or tm < ROW_BLOCK)
    tok = pl.BlockSpec((1, tm, NA_WIDTH), lambda b, i: (b, i, 0))
    return pl.pallas_call(
        functools.partial(_inproj_body, n_sub=max(tm // ROW_BLOCK, 1)),
        grid=(B, T // tm),
        in_specs=[
            pl.BlockSpec((1, tm, D), lambda b, i: (b, i, 0)),
            pl.BlockSpec((1, D), lambda b, i: (0, 0)),
            pl.BlockSpec(w.shape, lambda b, i: (0, 0)),
        ],
        out_specs=[tok] * 4,
        out_shape=[jax.ShapeDtypeStruct((B, T, NA_WIDTH), BF16)] * 3
        + [jax.ShapeDtypeStruct((B, T, FN_WIDTH), F32)],
        compiler_params=pltpu.CompilerParams(
            dimension_semantics=("arbitrary", "arbitrary"), vmem_limit_bytes=VMEM_LIMIT),
        name="inproj",
    )(x, g, w)


Q_ROWS = 8
Q_COLS = 16
K_ROWS = NA_WIN_ROWS + Q_ROWS - 1
K_COLS = 2 * NA_WIN_COLS
K_COL_START = (0, 8, 24, 32)
UNIT_KEYS = 512
UNIT_QUERIES = Q_ROWS * Q_COLS


def _attn_body(q_ref, k_ref, v_ref, km_ref, vm_ref, bias_ref, o_ref, *, rows, rows_per_step):
    chunk = pl.program_id(2)
    first_head = lax.broadcasted_iota(jnp.int32, (UNIT_QUERIES, LANES), 1) < NA_HEAD_DIM
    first_head_kv = lax.broadcasted_iota(jnp.int32, (UNIT_KEYS, LANES), 1) < NA_HEAD_DIM
    pad = jnp.zeros((UNIT_KEYS - K_ROWS * K_COLS - N_META, LANES), BF16)
    km = jnp.concatenate([km_ref[...], pad], axis=0)
    vm = jnp.concatenate([vm_ref[...], pad], axis=0)
    zero = jnp.zeros((UNIT_QUERIES, LANES), BF16)
    ones = jnp.ones((UNIT_KEYS, LANES), BF16)

    def window(ref, start_row, g):
        c0 = K_COL_START[g]
        pieces = []
        for w in range(K_ROWS):
            base = pl.multiple_of((start_row + w) * GRID_W, GRID_W)
            if c0 % 16 == 0:
                pieces.append(ref[0, pl.ds(base + c0, K_COLS), :])
            else:
                span = ref[0, pl.ds(base + c0 - SUBLANES, K_COLS + 16), :].astype(F32)
                pieces.append(span[SUBLANES:SUBLANES + K_COLS].astype(BF16))
        return pieces

    def scores(unit):
        blk, g = divmod(unit, GRID_W // Q_COLS)
        r0 = chunk * rows_per_step + Q_ROWS * blk
        start = jnp.clip(r0 - NA_WIN_ROWS // 2, 0, rows - K_ROWS)
        cfg = jnp.where(r0 == 0, 0, jnp.where(r0 == rows - Q_ROWS, 2, 1))
        q2 = jnp.concatenate(
            [q_ref[0, (Q_ROWS * blk + i) * GRID_W + Q_COLS * g:(Q_ROWS * blk + i) * GRID_W + Q_COLS * (g + 1), :]
             for i in range(Q_ROWS)], axis=0)
        qq = jnp.concatenate([jnp.where(first_head, q2, zero), jnp.where(first_head, zero, q2)], axis=0)
        kw = jnp.concatenate(window(k_ref, start, g) + [km], axis=0)
        s = lax.dot_general(qq, kw, (((1,), (1,)), ((), ())), preferred_element_type=F32)
        bias = jnp.concatenate([bias_ref[cfg, g, 0, 0], bias_ref[cfg, g, 0, 1]], axis=0)
        return s + bias, start

    def finish(unit, s, start):
        blk, g = divmod(unit, GRID_W // Q_COLS)
        vw = jnp.concatenate(window(v_ref, start, g) + [vm], axis=0)
        v0 = jnp.where(first_head_kv, vw, ones)
        v1 = jnp.where(first_head_kv, ones, vw)
        m = jnp.max(s, axis=-1, keepdims=True)
        p = jnp.exp2(s - m).astype(BF16)
        o0 = jnp.dot(p[:UNIT_QUERIES], v0, preferred_element_type=F32)
        o1 = jnp.dot(p[UNIT_QUERIES:], v1, preferred_element_type=F32)
        r0 = o0 / pltpu.roll(o0, NA_HEAD_DIM, axis=1)
        r1 = o1 / pltpu.roll(o1, NA_HEAD_DIM, axis=1)
        out = jnp.where(first_head, r0, r1).astype(BF16)
        for i in range(Q_ROWS):
            lo = (Q_ROWS * blk + i) * GRID_W + Q_COLS * g
            o_ref[0, lo:lo + Q_COLS, :] = out[i * Q_COLS:(i + 1) * Q_COLS]

    n_units = (rows_per_step // Q_ROWS) * (GRID_W // Q_COLS)
    queue = [scores(u) for u in range(min(SCORES_AHEAD, n_units))]
    for u in range(n_units):
        if u + SCORES_AHEAD < n_units:
            queue.append(scores(u + SCORES_AHEAD))
        finish(u, *queue.pop(0))


def _attention(q, k, v, km, vm, bias, rows_per_step):
    B, T, _ = q.shape
    rows = T // GRID_W
    assert T % GRID_W == 0 and rows % rows_per_step == 0 and rows_per_step % Q_ROWS == 0 and rows >= K_ROWS + Q_ROWS
    tq = rows_per_step * GRID_W
    body = functools.partial(_attn_body, rows=rows, rows_per_step=rows_per_step)
    return pl.pallas_call(
        body,
        grid=(B, HEAD_PAIRS, rows // rows_per_step),
        in_specs=[
            pl.BlockSpec((1, tq, LANES), lambda b, h, c: (b, c, h)),
            pl.BlockSpec((1, T, LANES), lambda b, h, c: (b, 0, h)),
            pl.BlockSpec((1, T, LANES), lambda b, h, c: (b, 0, h)),
            pl.BlockSpec((N_META, LANES), lambda b, h, c: (0, h)),
            pl.BlockSpec((N_META, LANES), lambda b, h, c: (0, h)),
            pl.BlockSpec(bias.shape[:2] + (1,) + bias.shape[3:], lambda b, h, c: (0, 0, h, 0, 0, 0)),
        ],
        out_specs=pl.BlockSpec((1, tq, LANES), lambda b, h, c: (b, c, h)),
        out_shape=jax.ShapeDtypeStruct((B, T, NA_WIDTH), BF16),
        compiler_params=pltpu.CompilerParams(
            dimension_semantics=("arbitrary", "arbitrary", "arbitrary"), vmem_limit_bytes=VMEM_LIMIT),
        name="attention",
    )(q, k, v, km, vm, bias)


UNIT_CONFIGS = ((0, tuple(max(i - 4, 0) for i in range(Q_ROWS))),
                (4, tuple(range(Q_ROWS))),
                (7, tuple(min(3 + i, 7) for i in range(Q_ROWS))))


def _bias_tables(rel_bias, meta_bias):
    n_groups = GRID_W // Q_COLS
    n_off = 2 * NA_WIN_COLS - 1
    n_row_off = 2 * NA_WIN_ROWS - 1
    hi = lax.Precision.HIGHEST
    onehot = np.zeros((n_groups, n_off, Q_COLS, K_COLS), np.float32)
    colmask = np.full((n_groups, Q_COLS, K_COLS), NEG, np.float32)
    for g in range(n_groups):
        for jq in range(Q_COLS):
            c = Q_COLS * g + jq
            cs = min(max(c - NA_WIN_COLS // 2, 0), GRID_W - NA_WIN_COLS)
            for x in range(K_COLS):
                j = K_COL_START[g] + x
                if cs <= j < cs + NA_WIN_COLS:
                    onehot[g, j - c + NA_WIN_COLS - 1, jq, x] = 1.0
                    colmask[g, jq, x] = 0.0
    blocks = jnp.einsum("hrd,gdjx->rghjx", rel_bias * LOG2E, onehot, precision=hi) + colmask[None, :, None]
    shape = (1, n_groups, NA_HEADS, Q_COLS, K_COLS)
    negblk = jnp.full(shape, NEG, F32)
    metablk = jnp.broadcast_to(jnp.concatenate(
        [meta_bias * LOG2E, jnp.full((NA_HEADS, K_COLS - N_META), NEG, F32)], axis=-1)[None, None, :, None, :], shape)
    blocks = jnp.concatenate([blocks, negblk, metablk], axis=0)
    slots = LANES // K_COLS
    placed = [jnp.pad(blocks, ((0, 0),) * 4 + ((K_COLS * kk, LANES - K_COLS * (kk + 1)),)) for kk in range(slots)]
    source = np.full((len(UNIT_CONFIGS), Q_ROWS, K_ROWS + 1), n_row_off, np.int32)
    for cc, (qslot, firsts) in enumerate(UNIT_CONFIGS):
        for i, w0 in enumerate(firsts):
            for w in range(w0, w0 + NA_WIN_ROWS):
                source[cc, i, w] = w - qslot - i + NA_WIN_ROWS - 1
            source[cc, i, K_ROWS] = n_row_off + 1

    def assemble(*refs):
        out_ref = refs[-1]
        for cc in range(source.shape[0]):
            for i in range(Q_ROWS):
                for vv in range((K_ROWS + 1) // slots):
                    acc = refs[0][int(source[cc, i, slots * vv])]
                    for kk in range(1, slots):
                        acc = acc + refs[kk][int(source[cc, i, slots * vv + kk])]
                    out_ref[cc, :, :, i * Q_COLS:(i + 1) * Q_COLS, vv * LANES:(vv + 1) * LANES] = acc

    t = pl.pallas_call(
        assemble,
        out_shape=jax.ShapeDtypeStruct((len(UNIT_CONFIGS), n_groups, NA_HEADS, UNIT_QUERIES, UNIT_KEYS), F32),
        compiler_params=pltpu.CompilerParams(vmem_limit_bytes=VMEM_LIMIT),
        name="bias_assemble",
    )(*placed)
    return t.reshape(t.shape[:2] + (HEAD_PAIRS, 2, UNIT_QUERIES, UNIT_KEYS))


def _snap(x):
    for v in (0.0, 1.0, -1.0):
        if abs(x - v) < 1e-12:
            return v
    return x


def _scaled(x, coef):
    if x is None or coef == 0.0:
        return None
    arr, sign = x
    coef = coef * sign
    if abs(coef) == 1.0:
        return arr, coef
    return arr * coef, 1.0


def _plus(a, b):
    if a is None:
        return b
    if b is None:
        return a
    (x, sx), (y, sy) = a, b
    if sx == sy:
        return x + y, sx
    return (x - y, 1.0) if sx > 0 else (y - x, 1.0)


def _negated(a):
    return None if a is None else (a[0], -a[1])


def _value(a):
    if a is None:
        return None
    return a[0] if a[1] > 0 else -a[0]


def _fft_real(xs):
    n = len(xs)
    if n == 1:
        return [(xs[0], None)]
    even, odd = _fft_real(xs[0::2]), _fft_real(xs[1::2])

    def bin_of(half, kk):
        if kk <= n // 4:
            return half[kk]
        re, im = half[n // 2 - kk]
        return re, _negated(im)

    out = []
    for kk in range(n // 2 + 1):
        c, s = _snap(math.cos(2 * math.pi * kk / n)), _snap(math.sin(2 * math.pi * kk / n))
        er, ei = bin_of(even, kk % (n // 2))
        orr, oi = bin_of(odd, kk % (n // 2))
        tr = _plus(_scaled(orr, c), _scaled(oi, s))
        ti = _plus(_scaled(oi, c), _scaled(orr, -s))
        out.append((_plus(er, tr), _plus(ei, ti)))
    return out


def _dft16_real(xs):
    return [(_value(re), _value(im)) for re, im in _fft_real([(x, 1.0) for x in xs])]


def _fourier_body(u_ref, um_ref, g_ref, gt_ref, twc_ref, tws_ref, cs_ref, f_ref,
                  zr_ref, zi_ref, sh_ref, *, n2):
    nm = n2 - 1
    half = DFT_OUTER // 2
    def stage1(r0, meta_rows):
        xs = [um_ref[pl.ds(r0, SUBLANES), :] if (a == 0 and meta_rows)
              else u_ref[0, pl.ds(n2 * a - N_META + r0, SUBLANES), :] for a in range(DFT_OUTER)]
        for c, (zr, zi) in enumerate(_dft16_real(xs)):
            zr_ref[c, pl.ds(r0, SUBLANES), :] = zr
            if zi is not None:
                zi_ref[c, pl.ds(r0, SUBLANES), :] = zi

    for i in range(N_META // SUBLANES):
        stage1(i * SUBLANES, True)

    def stage1_step(i, carry):
        stage1(pl.multiple_of(i * SUBLANES, SUBLANES), False)
        return carry

    lax.fori_loop(N_META // SUBLANES, nm // SUBLANES, stage1_step, 0, unroll=2)
    zt = _dft16_real([u_ref[0, pl.ds(n2 * a + nm - N_META, 1), :] for a in range(DFT_OUTER)])

    g = g_ref[...]
    gt = gt_ref[...]
    csm = cs_ref[...]
    inv_sqrt_n = 1.0 / math.sqrt(DFT_OUTER * n2)

    def prep(s):
        zr = zr_ref[s]
        ztr, zti = zt[s]
        zi = None if s in (0, half) else zi_ref[s]
        if s > 0:
            tc, ts = twc_ref[0:nm, s:s + 1], tws_ref[0:nm, s:s + 1]
            tct, tst = twc_ref[nm:n2, s:s + 1], tws_ref[nm:n2, s:s + 1]
            if zi is None:
                zr, zi = zr * tc, -(zr * ts)
                ztr, zti = ztr * tct, -(ztr * tst)
            else:
                zr, zi = zr * tc + zi * ts, zi * tc - zr * ts
                ztr, zti = ztr * tct + zti * tst, zti * tct - ztr * tst
        if zi is None:
            zi, zti = jnp.zeros_like(zr), jnp.zeros_like(ztr)
        z = jnp.concatenate([zr, zi], axis=1)
        tail = jnp.concatenate([ztr, zti], axis=1)
        return z, tail

    def dense(z, tail):
        p = jnp.dot(g, z.astype(BF16), preferred_element_type=F32)
        return p + gt * tail

    def channel_store(c, pr, pi):
        y = jnp.dot(jnp.concatenate([pr, pi], axis=1).astype(BF16), csm, preferred_element_type=F32)
        f_ref[0, pl.ds(c, nm, stride=DFT_OUTER), :] = y

    def post(s, p, z, tail):
        czr, czi, szr, szi = p[:nm, :LANES], p[:nm, LANES:], p[nm:, :LANES], p[nm:, LANES:]
        channel_store(s, czr + szi, czi - szr)
        if 0 < s < half:
            col = (jnp.sum(z, axis=0, keepdims=True) + tail) * inv_sqrt_n
            sh_ref[0, 0:nm] = czr - szi
            sh_ref[1, 0:nm] = -(czi + szr)
            sh_ref[0, nm:n2] = col[:, :LANES]
            sh_ref[1, nm:n2] = -col[:, LANES:]
            channel_store(DFT_OUTER - s, sh_ref[0, 1:n2], sh_ref[1, 1:n2])

    pending = None
    for s in range(half + 1):
        z, tail = prep(s)
        p = dense(z, tail)
        if pending is not None:
            post(*pending)
        pending = (s, p, z, tail)
    post(*pending)


def _fourier(u, um, g, gt, twc, tws, cs):
    B, T, _ = u.shape
    assert N_META == DFT_OUTER and T % (2 * SUBLANES * DFT_OUTER) == 0
    n2 = (N_META + T) // DFT_OUTER
    nm = n2 - 1
    body = functools.partial(_fourier_body, n2=n2)

    def const(arr):
        return pl.BlockSpec(arr.shape, lambda b, j: (0, 0))

    return pl.pallas_call(
        body,
        grid=(B, FN_GROUPS),
        in_specs=[
            pl.BlockSpec((1, T, LANES), lambda b, j: (b, 0, j)),
            pl.BlockSpec((N_META, LANES), lambda b, j: (0, j)),
            const(g), const(gt), const(twc), const(tws), const(cs),
        ],
        out_specs=pl.BlockSpec((1, T, LANES), lambda b, j: (b, 0, j)),
        out_shape=jax.ShapeDtypeStruct((B, T, FN_WIDTH), F32),
        scratch_shapes=[
            pltpu.VMEM((DFT_OUTER // 2 + 1, nm, LANES), F32),
            pltpu.VMEM((DFT_OUTER // 2 + 1, nm, LANES), F32),
            pltpu.VMEM((2, nm + SUBLANES, LANES), F32),
        ],
        compiler_params=pltpu.CompilerParams(
            dimension_semantics=("arbitrary", "arbitrary"), vmem_limit_bytes=VMEM_LIMIT),
        name="fourier",
    )(u, um, g, gt, twc, tws, cs)


def _fourier_constants(n2):
    n = DFT_OUTER * n2
    b = jnp.arange(n2, dtype=jnp.int32)
    d = jnp.arange(1, n2, dtype=jnp.int32)
    th = ((d[:, None] * b[None, :]) % n2).astype(F32) * (2.0 * math.pi / n2)
    full = jnp.concatenate([jnp.cos(th), jnp.sin(th)], axis=0) * (1.0 / math.sqrt(n))
    g, gt = full[:, :n2 - 1].astype(BF16), full[:, n2 - 1:]
    c = jnp.arange(DFT_OUTER, dtype=jnp.int32)
    tw = ((b[:, None] * c[None, :]) % n).astype(F32) * (2.0 * math.pi / n)
    return g, gt, jnp.cos(tw), jnp.sin(tw)


def _channel_dft_matrix():
    ch = jnp.arange(FN_GROUP_DIM, dtype=jnp.int32)
    th = ((ch[:, None] * ch[None, :]) % FN_GROUP_DIM).astype(F32) * (2.0 * math.pi / FN_GROUP_DIM)
    scale = 1.0 / math.sqrt(FN_GROUP_DIM)
    return jnp.concatenate([jnp.cos(th) * scale, jnp.sin(th) * scale], axis=0).astype(BF16)


def _trunk_body(x_ref, o_ref, f_ref, gmix_ref, wg_ref, wna_ref, wfn_ref, wout_ref, gmlp_ref,
                wup_ref, wdn_ref, gfin_ref, y_ref, *, ff_chunk, n_sub):
    def dot(a, b):
        return jnp.dot(a, b, preferred_element_type=F32)

    sub = x_ref.shape[1] // n_sub
    n_ff = D_FF // ff_chunk

    def chain(r):
        rows = slice(r * sub, (r + 1) * sub)
        x = x_ref[0, rows, :]
        h = _rms(x, gmix_ref[...]).astype(BF16)
        gates = dot(h, wg_ref[...])
        y_na = dot(o_ref[0, rows, :], wna_ref[...])
        y_fn = dot(f_ref[0, rows, :].astype(BF16), wfn_ref[...])
        yield
        mixed = jax.nn.sigmoid(gates[:, :D_MODEL]) * y_na + jax.nn.sigmoid(gates[:, D_MODEL:]) * y_fn
        x1 = x + dot(mixed.astype(BF16), wout_ref[...])
        yield
        h2 = _rms(x1, gmlp_ref[...]).astype(BF16)
        acc = x1
        up = dot(h2, wup_ref[:, 0:ff_chunk])
        yield
        for c in range(n_ff):
            a = jnp.maximum(up, 0.0)
            a = (a * a).astype(BF16)
            if c + 1 < n_ff:
                up = dot(h2, wup_ref[:, (c + 1) * ff_chunk:(c + 2) * ff_chunk])
            acc = acc + dot(a, wdn_ref[c * ff_chunk:(c + 1) * ff_chunk, :])
            yield
        y_ref[0, rows, :] = _rms(acc, gfin_ref[...])

    live = [chain(r) for r in range(n_sub)]
    while live:
        for g in list(live):
            try:
                next(g)
            except StopIteration:
                live.remove(g)


def _trunk(x, o, f, gmix, wg, wna, wfn, wout, gmlp, wup, wdn, gfin, tm):
    B, T, D = x.shape
    assert T % tm == 0 and tm % ROW_BLOCK == 0 and D_FF % FF_CHUNK == 0
    body = functools.partial(_trunk_body, ff_chunk=FF_CHUNK, n_sub=tm // ROW_BLOCK)

    def const(arr):
        return pl.BlockSpec(arr.shape, lambda b, i: (0, 0), pipeline_mode=pl.Buffered(1))

    return pl.pallas_call(
        body,
        grid=(B, T // tm),
        in_specs=[
            pl.BlockSpec((1, tm, D), lambda b, i: (b, i, 0)),
            pl.BlockSpec((1, tm, NA_WIDTH), lambda b, i: (b, i, 0)),
            pl.BlockSpec((1, tm, FN_WIDTH), lambda b, i: (b, i, 0)),
            const(gmix), const(wg), const(wna), const(wfn), const(wout), const(gmlp),
            const(wup), const(wdn), const(gfin),
        ],
        out_specs=pl.BlockSpec((1, tm, D), lambda b, i: (b, i, 0)),
        out_shape=jax.ShapeDtypeStruct((B, T, D), F32),
        compiler_params=pltpu.CompilerParams(
            dimension_semantics=("arbitrary", "arbitrary"), vmem_limit_bytes=VMEM_LIMIT),
        name="trunk",
    )(x, o, f, gmix, wg, wna, wfn, wout, gmlp, wup, wdn, gfin)


def _run_group(x, meta, consts):
    (w_qkvu, cs, bias, g_mix, wg, wna, wfn, wout, g_mlp, wup, wdn, g_fin) = consts
    B, T, _ = x.shape
    n2 = (N_META + T) // DFT_OUTER
    q, k, v, u = _inproj(x, g_mix, w_qkvu, tm=INPROJ_TOKENS)
    _, km, vm, um = meta
    o = _attention(q, k, v, km, vm, bias, rows_per_step=ATTN_ROWS_PER_STEP)
    f = _fourier(u, um, *_fourier_constants(n2), cs)
    return _trunk(x, o, f, g_mix, wg, wna, wfn, wout, g_mlp, wup, wdn, g_fin, tm=TRUNK_TOKENS)


def kernel(x_prompt, x_sample, meta_tokens, w_in, rel_bias, meta_bias, w_branch_na, w_branch_fn,
           w_out, g_mix, g_mlp, w_up, w_down, g_final):
    assert w_in.shape[0] == 1, "single-layer trunk"
    w_in0 = w_in[0]
    split = 3 * NA_WIDTH + FN_WIDTH
    w_qkvu = w_in0[:, :split].astype(BF16)
    wg = w_in0[:, split:].astype(BF16)
    cs = _channel_dft_matrix()
    bias = _bias_tables(rel_bias[0], meta_bias[0])
    g_mix2 = g_mix[0][None]
    consts = (w_qkvu, cs, bias, g_mix2, wg, w_branch_na[0].astype(BF16), w_branch_fn[0].astype(BF16),
              w_out[0].astype(BF16), g_mlp[0][None], w_up[0].astype(BF16), w_down[0].astype(BF16),
              g_final[None])
    meta = _inproj(meta_tokens[None], g_mix2, w_qkvu, tm=N_META)
    meta = tuple(m[0] for m in meta)
    return (_run_group(x_prompt, meta, consts), _run_group(x_sample, meta, consts))
```

```python
import functools
import math

import jax
import jax.numpy as jnp
import numpy as np
from jax import lax
from jax.experimental import pallas as pl
from jax.experimental.pallas import tpu as pltpu

F32 = jnp.float32
BF16 = jnp.bfloat16

D_MODEL = 1024
N_META = 16
GRID_W = 64
NA_HEADS = 8
NA_HEAD_DIM = 64
NA_WIDTH = NA_HEADS * NA_HEAD_DIM
NA_WIN_ROWS = 8
NA_WIN_COLS = 16
FN_GROUPS = 4
FN_GROUP_DIM = 128
FN_WIDTH = FN_GROUPS * FN_GROUP_DIM
D_FF = 4 * D_MODEL
RMS_EPS = 1e-6

LANES = 128
SUBLANES = 8
HEAD_PAIRS = NA_HEADS // 2
SCORES_AHEAD = 2
ROW_BLOCK = 256
NEG = -1e30
LOG2E = math.log2(math.e)
DFT_OUTER = 16
VMEM_LIMIT = 56 * 1024 * 1024
INPROJ_TOKENS = 4 * ROW_BLOCK
TRUNK_TOKENS = 2 * ROW_BLOCK
ATTN_ROWS_PER_STEP = 64
FF_CHUNK = 1024


def _rms(x, g):
    ms = jnp.mean(x * x, axis=-1, keepdims=True)
    return x * lax.rsqrt(ms + RMS_EPS) * g


def _inproj_body(x_ref, g_ref, w_ref, *refs, n_sub, n_cast):
    cast_in, (q_ref, k_ref, v_ref, u_ref), cast_out = refs[:n_cast], refs[n_cast:n_cast + 4], refs[n_cast + 4:]
    sub = x_ref.shape[1] // n_sub
    for r in range(n_sub):
        rows = slice(r * sub, (r + 1) * sub)
        h = _rms(x_ref[0, rows, :], g_ref[...]).astype(BF16)
        z = jnp.dot(h, w_ref[...], preferred_element_type=F32)
        q_ref[0, rows, :] = (z[:, :NA_WIDTH] * (NA_HEAD_DIM ** -0.5 * LOG2E)).astype(BF16)
        k_ref[0, rows, :] = z[:, NA_WIDTH:2 * NA_WIDTH].astype(BF16)
        v_ref[0, rows, :] = z[:, 2 * NA_WIDTH:3 * NA_WIDTH].astype(BF16)
        u_ref[0, rows, :] = z[:, 3 * NA_WIDTH:]
    for src, dst in zip(cast_in, cast_out):
        dst[...] = src[...].astype(BF16)


def _inproj(x, g, w, tm, cast=()):
    B, T, D = x.shape
    assert T % tm == 0 and (tm % ROW_BLOCK == 0 or tm < ROW_BLOCK)
    steps = B * (T // tm)
    tok = pl.BlockSpec((1, tm, NA_WIDTH), lambda b, i: (b, i, 0))
    cast_specs, cast_shapes = [], []
    for mat, width, col in cast:
        rows = mat.shape[0] // steps
        assert mat.shape[0] % steps == 0 and rows % 16 == 0 and width % LANES == 0
        cast_specs.append(pl.BlockSpec((rows, width), lambda b, i, col=col: (b * (T // tm) + i, col)))
        cast_shapes.append(jax.ShapeDtypeStruct((mat.shape[0], width), BF16))
    out_cast_specs = [pl.BlockSpec(s.block_shape, lambda b, i: (b * (T // tm) + i, 0)) for s in cast_specs]
    outs = pl.pallas_call(
        functools.partial(_inproj_body, n_sub=max(tm // ROW_BLOCK, 1), n_cast=len(cast)),
        grid=(B, T // tm),
        in_specs=[
            pl.BlockSpec((1, tm, D), lambda b, i: (b, i, 0)),
            pl.BlockSpec((1, D), lambda b, i: (0, 0)),
            pl.BlockSpec(w.shape, lambda b, i: (0, 0)),
        ] + cast_specs,
        out_specs=[tok] * 4 + out_cast_specs,
        out_shape=[jax.ShapeDtypeStruct((B, T, NA_WIDTH), BF16)] * 3
        + [jax.ShapeDtypeStruct((B, T, FN_WIDTH), F32)] + cast_shapes,
        compiler_params=pltpu.CompilerParams(
            dimension_semantics=("arbitrary", "arbitrary"), vmem_limit_bytes=VMEM_LIMIT),
        name="inproj",
    )(x, g, w, *[mat for mat, _, _ in cast])
    return tuple(outs[:4]), tuple(outs[4:])


Q_ROWS = 8
Q_COLS = 16
K_ROWS = NA_WIN_ROWS + Q_ROWS - 1
K_COLS = 2 * NA_WIN_COLS
K_COL_START = (0, 8, 24, 32)
UNIT_KEYS = 512
UNIT_QUERIES = Q_ROWS * Q_COLS


def _attn_body(q_ref, k_ref, v_ref, km_ref, vm_ref, bias_ref, o_ref, *, rows, rows_per_step):
    chunk = pl.program_id(2)
    first_head = lax.broadcasted_iota(jnp.int32, (UNIT_QUERIES, LANES), 1) < NA_HEAD_DIM
    first_head_kv = lax.broadcasted_iota(jnp.int32, (UNIT_KEYS, LANES), 1) < NA_HEAD_DIM
    pad = jnp.zeros((UNIT_KEYS - K_ROWS * K_COLS - N_META, LANES), BF16)
    km = jnp.concatenate([km_ref[...], pad], axis=0)
    vm = jnp.concatenate([vm_ref[...], pad], axis=0)
    zero = jnp.zeros((UNIT_QUERIES, LANES), BF16)
    ones = jnp.ones((UNIT_KEYS, LANES), BF16)

    def window(ref, start_row, g):
        c0 = K_COL_START[g]
        pieces = []
        for w in range(K_ROWS):
            base = pl.multiple_of((start_row + w) * GRID_W, GRID_W)
            if c0 % 16 == 0:
                pieces.append(ref[0, pl.ds(base + c0, K_COLS), :])
            else:
                span = ref[0, pl.ds(base + c0 - SUBLANES, K_COLS + 16), :].astype(F32)
                pieces.append(span[SUBLANES:SUBLANES + K_COLS].astype(BF16))
        return pieces

    def scores(unit):
        blk, g = divmod(unit, GRID_W // Q_COLS)
        r0 = chunk * rows_per_step + Q_ROWS * blk
        start = jnp.clip(r0 - NA_WIN_ROWS // 2, 0, rows - K_ROWS)
        cfg = jnp.where(r0 == 0, 0, jnp.where(r0 == rows - Q_ROWS, 2, 1))
        q2 = jnp.concatenate(
            [q_ref[0, (Q_ROWS * blk + i) * GRID_W + Q_COLS * g:(Q_ROWS * blk + i) * GRID_W + Q_COLS * (g + 1), :]
             for i in range(Q_ROWS)], axis=0)
        qq = jnp.concatenate([jnp.where(first_head, q2, zero), jnp.where(first_head, zero, q2)], axis=0)
        kw = jnp.concatenate(window(k_ref, start, g) + [km], axis=0)
        s = lax.dot_general(qq, kw, (((1,), (1,)), ((), ())), preferred_element_type=F32)
        bias = jnp.concatenate([bias_ref[cfg, g, 0, 0], bias_ref[cfg, g, 0, 1]], axis=0)
        return s + bias, start

    def finish(unit, s, start):
        blk, g = divmod(unit, GRID_W // Q_COLS)
        vw = jnp.concatenate(window(v_ref, start, g) + [vm], axis=0)
        v0 = jnp.where(first_head_kv, vw, ones)
        v1 = jnp.where(first_head_kv, ones, vw)
        m = jnp.max(s, axis=-1, keepdims=True)
        p = jnp.exp2(s - m).astype(BF16)
        o0 = jnp.dot(p[:UNIT_QUERIES], v0, preferred_element_type=F32)
        o1 = jnp.dot(p[UNIT_QUERIES:], v1, preferred_element_type=F32)
        r0 = o0 / pltpu.roll(o0, NA_HEAD_DIM, axis=1)
        r1 = o1 / pltpu.roll(o1, NA_HEAD_DIM, axis=1)
        out = jnp.where(first_head, r0, r1).astype(BF16)
        for i in range(Q_ROWS):
            lo = (Q_ROWS * blk + i) * GRID_W + Q_COLS * g
            o_ref[0, lo:lo + Q_COLS, :] = out[i * Q_COLS:(i + 1) * Q_COLS]

    n_units = (rows_per_step // Q_ROWS) * (GRID_W // Q_COLS)
    queue = [scores(u) for u in range(min(SCORES_AHEAD, n_units))]
    for u in range(n_units):
        if u + SCORES_AHEAD < n_units:
            queue.append(scores(u + SCORES_AHEAD))
        finish(u, *queue.pop(0))


def _attention(q, k, v, km, vm, bias, rows_per_step):
    B, T, _ = q.shape
    rows = T // GRID_W
    assert T % GRID_W == 0 and rows % rows_per_step == 0 and rows_per_step % Q_ROWS == 0 and rows >= K_ROWS + Q_ROWS
    tq = rows_per_step * GRID_W
    body = functools.partial(_attn_body, rows=rows, rows_per_step=rows_per_step)
    return pl.pallas_call(
        body,
        grid=(B, HEAD_PAIRS, rows // rows_per_step),
        in_specs=[
            pl.BlockSpec((1, tq, LANES), lambda b, h, c: (b, c, h)),
            pl.BlockSpec((1, T, LANES), lambda b, h, c: (b, 0, h)),
            pl.BlockSpec((1, T, LANES), lambda b, h, c: (b, 0, h)),
            pl.BlockSpec((N_META, LANES), lambda b, h, c: (0, h)),
            pl.BlockSpec((N_META, LANES), lambda b, h, c: (0, h)),
            pl.BlockSpec(bias.shape[:2] + (1,) + bias.shape[3:], lambda b, h, c: (0, 0, h, 0, 0, 0)),
        ],
        out_specs=pl.BlockSpec((1, tq, LANES), lambda b, h, c: (b, c, h)),
        out_shape=jax.ShapeDtypeStruct((B, T, NA_WIDTH), BF16),
        compiler_params=pltpu.CompilerParams(
            dimension_semantics=("arbitrary", "arbitrary", "arbitrary"), vmem_limit_bytes=VMEM_LIMIT),
        name="attention",
    )(q, k, v, km, vm, bias)


UNIT_CONFIGS = ((0, tuple(max(i - 4, 0) for i in range(Q_ROWS))),
                (4, tuple(range(Q_ROWS))),
                (7, tuple(min(3 + i, 7) for i in range(Q_ROWS))))


def _bias_tables(rel_bias, meta_bias):
    n_groups = GRID_W // Q_COLS
    n_off = 2 * NA_WIN_COLS - 1
    n_row_off = 2 * NA_WIN_ROWS - 1
    hi = lax.Precision.HIGHEST
    onehot = np.zeros((n_groups, n_off, Q_COLS, K_COLS), np.float32)
    colmask = np.full((n_groups, Q_COLS, K_COLS), NEG, np.float32)
    for g in range(n_groups):
        for jq in range(Q_COLS):
            c = Q_COLS * g + jq
            cs = min(max(c - NA_WIN_COLS // 2, 0), GRID_W - NA_WIN_COLS)
            for x in range(K_COLS):
                j = K_COL_START[g] + x
                if cs <= j < cs + NA_WIN_COLS:
                    onehot[g, j - c + NA_WIN_COLS - 1, jq, x] = 1.0
                    colmask[g, jq, x] = 0.0
    blocks = jnp.einsum("hrd,gdjx->rghjx", rel_bias * LOG2E, onehot, precision=hi) + colmask[None, :, None]
    shape = (1, n_groups, NA_HEADS, Q_COLS, K_COLS)
    negblk = jnp.full(shape, NEG, F32)
    metablk = jnp.broadcast_to(jnp.concatenate(
        [meta_bias * LOG2E, jnp.full((NA_HEADS, K_COLS - N_META), NEG, F32)], axis=-1)[None, None, :, None, :], shape)
    blocks = jnp.concatenate([blocks, negblk, metablk], axis=0)
    slots = LANES // K_COLS
    placed = [jnp.pad(blocks, ((0, 0),) * 4 + ((K_COLS * kk, LANES - K_COLS * (kk + 1)),)) for kk in range(slots)]
    source = np.full((len(UNIT_CONFIGS), Q_ROWS, K_ROWS + 1), n_row_off, np.int32)
    for cc, (qslot, firsts) in enumerate(UNIT_CONFIGS):
        for i, w0 in enumerate(firsts):
            for w in range(w0, w0 + NA_WIN_ROWS):
                source[cc, i, w] = w - qslot - i + NA_WIN_ROWS - 1
            source[cc, i, K_ROWS] = n_row_off + 1

    def assemble(*refs):
        out_ref = refs[-1]
        for cc in range(source.shape[0]):
            for i in range(Q_ROWS):
                for vv in range((K_ROWS + 1) // slots):
                    acc = refs[0][int(source[cc, i, slots * vv])]
                    for kk in range(1, slots):
                        acc = acc + refs[kk][int(source[cc, i, slots * vv + kk])]
                    out_ref[cc, :, :, i * Q_COLS:(i + 1) * Q_COLS, vv * LANES:(vv + 1) * LANES] = acc

    t = pl.pallas_call(
        assemble,
        out_shape=jax.ShapeDtypeStruct((len(UNIT_CONFIGS), n_groups, NA_HEADS, UNIT_QUERIES, UNIT_KEYS), F32),
        compiler_params=pltpu.CompilerParams(vmem_limit_bytes=VMEM_LIMIT),
        name="bias_assemble",
    )(*placed)
    return t.reshape(t.shape[:2] + (HEAD_PAIRS, 2, UNIT_QUERIES, UNIT_KEYS))


def _snap(x):
    for v in (0.0, 1.0, -1.0):
        if abs(x - v) < 1e-12:
            return v
    return x


def _scaled(x, coef):
    if x is None or coef == 0.0:
        return None
    arr, sign = x
    coef = coef * sign
    if abs(coef) == 1.0:
        return arr, coef
    return arr * coef, 1.0


def _plus(a, b):
    if a is None:
        return b
    if b is None:
        return a
    (x, sx), (y, sy) = a, b
    if sx == sy:
        return x + y, sx
    return (x - y, 1.0) if sx > 0 else (y - x, 1.0)


def _negated(a):
    return None if a is None else (a[0], -a[1])


def _value(a):
    if a is None:
        return None
    return a[0] if a[1] > 0 else -a[0]


def _fft_real(xs):
    n = len(xs)
    if n == 1:
        return [(xs[0], None)]
    even, odd = _fft_real(xs[0::2]), _fft_real(xs[1::2])

    def bin_of(half, kk):
        if kk <= n // 4:
            return half[kk]
        re, im = half[n // 2 - kk]
        return re, _negated(im)

    out = []
    for kk in range(n // 2 + 1):
        c, s = _snap(math.cos(2 * math.pi * kk / n)), _snap(math.sin(2 * math.pi * kk / n))
        er, ei = bin_of(even, kk % (n // 2))
        orr, oi = bin_of(odd, kk % (n // 2))
        tr = _plus(_scaled(orr, c), _scaled(oi, s))
        ti = _plus(_scaled(oi, c), _scaled(orr, -s))
        out.append((_plus(er, tr), _plus(ei, ti)))
    return out


def _dft16_real(xs):
    return [(_value(re), _value(im)) for re, im in _fft_real([(x, 1.0) for x in xs])]


def _fourier_body(u_ref, um_ref, g_ref, gt_ref, twc_ref, tws_ref, cs_ref, f_ref,
                  zr_ref, zi_ref, sh_ref, *, n2):
    nm = n2 - 1
    half = DFT_OUTER // 2
    def stage1(r0, meta_rows):
        xs = [um_ref[pl.ds(r0, SUBLANES), :] if (a == 0 and meta_rows)
              else u_ref[0, pl.ds(n2 * a - N_META + r0, SUBLANES), :] for a in range(DFT_OUTER)]
        for c, (zr, zi) in enumerate(_dft16_real(xs)):
            zr_ref[c, pl.ds(r0, SUBLANES), :] = zr
            if zi is not None:
                zi_ref[c, pl.ds(r0, SUBLANES), :] = zi

    for i in range(N_META // SUBLANES):
        stage1(i * SUBLANES, True)

    def stage1_step(i, carry):
        stage1(pl.multiple_of(i * SUBLANES, SUBLANES), False)
        return carry

    lax.fori_loop(N_META // SUBLANES, nm // SUBLANES, stage1_step, 0, unroll=2)
    zt = _dft16_real([u_ref[0, pl.ds(n2 * a + nm - N_META, 1), :] for a in range(DFT_OUTER)])

    g = g_ref[...]
    gt = gt_ref[...]
    csm = cs_ref[...]
    inv_sqrt_n = 1.0 / math.sqrt(DFT_OUTER * n2)

    def prep(s):
        zr = zr_ref[s]
        ztr, zti = zt[s]
        zi = None if s in (0, half) else zi_ref[s]
        if s > 0:
            tc, ts = twc_ref[0:nm, s:s + 1], tws_ref[0:nm, s:s + 1]
            tct, tst = twc_ref[nm:n2, s:s + 1], tws_ref[nm:n2, s:s + 1]
            if zi is None:
                zr, zi = zr * tc, -(zr * ts)
                ztr, zti = ztr * tct, -(ztr * tst)
            else:
                zr, zi = zr * tc + zi * ts, zi * tc - zr * ts
                ztr, zti = ztr * tct + zti * tst, zti * tct - ztr * tst
        if zi is None:
            zi, zti = jnp.zeros_like(zr), jnp.zeros_like(ztr)
        z = jnp.concatenate([zr, zi], axis=1)
        tail = jnp.concatenate([ztr, zti], axis=1)
        return z, tail

    def dense(z, tail):
        p = jnp.dot(g, z.astype(BF16), preferred_element_type=F32)
        return p + gt * tail

    def channel_store(c, pr, pi):
        y = jnp.dot(jnp.concatenate([pr, pi], axis=1).astype(BF16), csm, preferred_element_type=F32)
        f_ref[0, pl.ds(c, nm, stride=DFT_OUTER), :] = y

    def post(s, p, z, tail):
        czr, czi, szr, szi = p[:nm, :LANES], p[:nm, LANES:], p[nm:, :LANES], p[nm:, LANES:]
        channel_store(s, czr + szi, czi - szr)
        if 0 < s < half:
            col = (jnp.sum(z, axis=0, keepdims=True) + tail) * inv_sqrt_n
            sh_ref[0, 0:nm] = czr - szi
            sh_ref[1, 0:nm] = -(czi + szr)
            sh_ref[0, nm:n2] = col[:, :LANES]
            sh_ref[1, nm:n2] = -col[:, LANES:]
            channel_store(DFT_OUTER - s, sh_ref[0, 1:n2], sh_ref[1, 1:n2])

    pending = None
    for s in range(half + 1):
        z, tail = prep(s)
        p = dense(z, tail)
        if pending is not None:
            post(*pending)
        pending = (s, p, z, tail)
    post(*pending)


def _fourier(u, um, g, gt, twc, tws, cs):
    B, T, _ = u.shape
    assert N_META == DFT_OUTER and T % (2 * SUBLANES * DFT_OUTER) == 0
    n2 = (N_META + T) // DFT_OUTER
    nm = n2 - 1
    body = functools.partial(_fourier_body, n2=n2)

    def const(arr):
        return pl.BlockSpec(arr.shape, lambda b, j: (0, 0))

    return pl.pallas_call(
        body,
        grid=(B, FN_GROUPS),
        in_specs=[
            pl.BlockSpec((1, T, LANES), lambda b, j: (b, 0, j)),
            pl.BlockSpec((N_META, LANES), lambda b, j: (0, j)),
            const(g), const(gt), const(twc), const(tws), const(cs),
        ],
        out_specs=pl.BlockSpec((1, T, LANES), lambda b, j: (b, 0, j)),
        out_shape=jax.ShapeDtypeStruct((B, T, FN_WIDTH), F32),
        scratch_shapes=[
            pltpu.VMEM((DFT_OUTER // 2 + 1, nm, LANES), F32),
            pltpu.VMEM((DFT_OUTER // 2 + 1, nm, LANES), F32),
            pltpu.VMEM((2, nm + SUBLANES, LANES), F32),
        ],
        compiler_params=pltpu.CompilerParams(
            dimension_semantics=("arbitrary", "arbitrary"), vmem_limit_bytes=VMEM_LIMIT),
        name="fourier",
    )(u, um, g, gt, twc, tws, cs)


def _fourier_constants(n2):
    n = DFT_OUTER * n2
    b = jnp.arange(n2, dtype=jnp.int32)
    d = jnp.arange(1, n2, dtype=jnp.int32)
    th = ((d[:, None] * b[None, :]) % n2).astype(F32) * (2.0 * math.pi / n2)
    full = jnp.concatenate([jnp.cos(th), jnp.sin(th)], axis=0) * (1.0 / math.sqrt(n))
    g, gt = full[:, :n2 - 1].astype(BF16), full[:, n2 - 1:]
    c = jnp.arange(DFT_OUTER, dtype=jnp.int32)
    tw = ((b[:, None] * c[None, :]) % n).astype(F32) * (2.0 * math.pi / n)
    return g, gt, jnp.cos(tw), jnp.sin(tw)


def _channel_dft_matrix():
    ch = jnp.arange(FN_GROUP_DIM, dtype=jnp.int32)
    th = ((ch[:, None] * ch[None, :]) % FN_GROUP_DIM).astype(F32) * (2.0 * math.pi / FN_GROUP_DIM)
    scale = 1.0 / math.sqrt(FN_GROUP_DIM)
    return jnp.concatenate([jnp.cos(th) * scale, jnp.sin(th) * scale], axis=0).astype(BF16)


def _trunk_body(x_ref, o_ref, f_ref, gmix_ref, wg_ref, wna_ref, wfn_ref, wout_ref, gmlp_ref,
                wup_ref, wdn_ref, gfin_ref, y_ref, *, ff_chunk, n_sub):
    def dot(a, b):
        return jnp.dot(a, b, preferred_element_type=F32)

    sub = x_ref.shape[1] // n_sub
    n_ff = D_FF // ff_chunk

    def chain(r):
        rows = slice(r * sub, (r + 1) * sub)
        x = x_ref[0, rows, :]
        h = _rms(x, gmix_ref[...]).astype(BF16)
        gates = dot(h, wg_ref[...])
        y_na = dot(o_ref[0, rows, :], wna_ref[...])
        y_fn = dot(f_ref[0, rows, :].astype(BF16), wfn_ref[...])
        yield
        mixed = jax.nn.sigmoid(gates[:, :D_MODEL]) * y_na + jax.nn.sigmoid(gates[:, D_MODEL:]) * y_fn
        x1 = x + dot(mixed.astype(BF16), wout_ref[...])
        yield
        h2 = _rms(x1, gmlp_ref[...]).astype(BF16)
        acc = x1
        up = dot(h2, wup_ref[:, 0:ff_chunk])
        yield
        for c in range(n_ff):
            a = jnp.maximum(up, 0.0)
            a = (a * a).astype(BF16)
            if c + 1 < n_ff:
                up = dot(h2, wup_ref[:, (c + 1) * ff_chunk:(c + 2) * ff_chunk])
            acc = acc + dot(a, wdn_ref[c * ff_chunk:(c + 1) * ff_chunk, :])
            yield
        y_ref[0, rows, :] = _rms(acc, gfin_ref[...])

    live = [chain(r) for r in range(n_sub)]
    while live:
        for g in list(live):
            try:
                next(g)
            except StopIteration:
                live.remove(g)


def _trunk(x, o, f, gmix, wg, wna, wfn, wout, gmlp, wup, wdn, gfin, tm):
    B, T, D = x.shape
    assert T % tm == 0 and tm % ROW_BLOCK == 0 and D_FF % FF_CHUNK == 0
    body = functools.partial(_trunk_body, ff_chunk=FF_CHUNK, n_sub=tm // ROW_BLOCK)

    def const(arr):
        return pl.BlockSpec(arr.shape, lambda b, i: (0, 0), pipeline_mode=pl.Buffered(1))

    return pl.pallas_call(
        body,
        grid=(B, T // tm),
        in_specs=[
            pl.BlockSpec((1, tm, D), lambda b, i: (b, i, 0)),
            pl.BlockSpec((1, tm, NA_WIDTH), lambda b, i: (b, i, 0)),
            pl.BlockSpec((1, tm, FN_WIDTH), lambda b, i: (b, i, 0)),
            const(gmix), const(wg), const(wna), const(wfn), const(wout), const(gmlp),
            const(wup), const(wdn), const(gfin),
        ],
        out_specs=pl.BlockSpec((1, tm, D), lambda b, i: (b, i, 0)),
        out_shape=jax.ShapeDtypeStruct((B, T, D), F32),
        compiler_params=pltpu.CompilerParams(
            dimension_semantics=("arbitrary", "arbitrary"), vmem_limit_bytes=VMEM_LIMIT),
        name="trunk",
    )(x, o, f, gmix, wg, wna, wfn, wout, gmlp, wup, wdn, gfin)


def _run_group(x, qkvu, meta, consts):
    (cs, bias, g_mix, wg, wna, wfn, wout, g_mlp, wup, wdn, g_fin) = consts
    B, T, _ = x.shape
    n2 = (N_META + T) // DFT_OUTER
    q, k, v, u = qkvu
    _, km, vm, um = meta
    o = _attention(q, k, v, km, vm, bias, rows_per_step=ATTN_ROWS_PER_STEP)
    f = _fourier(u, um, *_fourier_constants(n2), cs)
    return _trunk(x, o, f, g_mix, wg, wna, wfn, wout, g_mlp, wup, wdn, g_fin, tm=TRUNK_TOKENS)


def kernel(x_prompt, x_sample, meta_tokens, w_in, rel_bias, meta_bias, w_branch_na, w_branch_fn,
           w_out, g_mix, g_mlp, w_up, w_down, g_final):
    assert w_in.shape[0] == 1, "single-layer trunk"
    w_in0 = w_in[0]
    split = 3 * NA_WIDTH + FN_WIDTH
    w_qkvu = w_in0[:, :split].astype(BF16)
    cs = _channel_dft_matrix()
    bias = _bias_tables(rel_bias[0], meta_bias[0])
    g_mix2 = g_mix[0][None]
    meta, _ = _inproj(meta_tokens[None], g_mix2, w_qkvu, tm=N_META)
    meta = tuple(m[0] for m in meta)
    later = [(w_in0, 2 * D_MODEL, split // (2 * D_MODEL)), (w_branch_na[0], D_MODEL, 0), (w_branch_fn[0], D_MODEL, 0),
             (w_out[0], D_MODEL, 0), (w_up[0], D_FF, 0), (w_down[0], D_MODEL, 0)]
    qkvu_prompt, (wg, wna, wfn, wout, wup, wdn) = _inproj(x_prompt, g_mix2, w_qkvu, tm=INPROJ_TOKENS, cast=later)
    qkvu_sample, _ = _inproj(x_sample, g_mix2, w_qkvu, tm=INPROJ_TOKENS)
    consts = (cs, bias, g_mix2, wg, wna, wfn, wout, g_mlp[0][None], wup, wdn, g_final[None])
    return (_run_group(x_prompt, qkvu_prompt, meta, consts), _run_group(x_sample, qkvu_sample, meta, consts))
```

```python
import functools
import math

import jax
import jax.numpy as jnp
import numpy as np
from jax import lax
from jax.experimental import pallas as pl
from jax.experimental.pallas import tpu as pltpu

F32 = jnp.float32
BF16 = jnp.bfloat16

D_MODEL = 1024
N_META = 16
GRID_W = 64
NA_HEADS = 8
NA_HEAD_DIM = 64
NA_WIDTH = NA_HEADS * NA_HEAD_DIM
NA_WIN_ROWS = 8
NA_WIN_COLS = 16
FN_GROUPS = 4
FN_GROUP_DIM = 128
FN_WIDTH = FN_GROUPS * FN_GROUP_DIM
D_FF = 4 * D_MODEL
RMS_EPS = 1e-6

LANES = 128
SUBLANES = 8
HEAD_PAIRS = NA_HEADS // 2
SCORES_AHEAD = 2
ROW_BLOCK = 256
NEG = -1e30
LOG2E = math.log2(math.e)
DFT_OUTER = 16
VMEM_LIMIT = 56 * 1024 * 1024
INPROJ_TOKENS = 8 * ROW_BLOCK
TRUNK_TOKENS = 4 * ROW_BLOCK
ATTN_ROWS_PER_STEP = 128
FF_CHUNK = 512
TRUNK_VMEM_LIMIT = 60 * 1024 * 1024


def _rms(x, g):
    ms = jnp.mean(x * x, axis=-1, keepdims=True)
    return x * lax.rsqrt(ms + RMS_EPS) * g


def _inproj_body(x_ref, g_ref, w_ref, *refs, n_sub, n_cast):
    cast_in, (q_ref, k_ref, v_ref, u_ref), cast_out = refs[:n_cast], refs[n_cast:n_cast + 4], refs[n_cast + 4:]
    sub = x_ref.shape[1] // n_sub
    for r in range(n_sub):
        rows = slice(r * sub, (r + 1) * sub)
        h = _rms(x_ref[0, rows, :], g_ref[...]).astype(BF16)
        z = jnp.dot(h, w_ref[...], preferred_element_type=F32)
        q_ref[0, rows, :] = (z[:, :NA_WIDTH] * (NA_HEAD_DIM ** -0.5 * LOG2E)).astype(BF16)
        k_ref[0, rows, :] = z[:, NA_WIDTH:2 * NA_WIDTH].astype(BF16)
        v_ref[0, rows, :] = z[:, 2 * NA_WIDTH:3 * NA_WIDTH].astype(BF16)
        u_ref[0, rows, :] = z[:, 3 * NA_WIDTH:]
    for src, dst in zip(cast_in, cast_out):
        dst[...] = src[...].astype(BF16)


def _inproj(x, g, w, tm, cast=()):
    B, T, D = x.shape
    assert T % tm == 0 and (tm % ROW_BLOCK == 0 or tm < ROW_BLOCK)
    steps = B * (T // tm)
    tok = pl.BlockSpec((1, tm, NA_WIDTH), lambda b, i: (b, i, 0))
    cast_specs, cast_shapes = [], []
    for mat, width, col in cast:
        rows = mat.shape[0] // steps
        assert mat.shape[0] % steps == 0 and rows % 16 == 0 and width % LANES == 0
        cast_specs.append(pl.BlockSpec((rows, width), lambda b, i, col=col: (b * (T // tm) + i, col)))
        cast_shapes.append(jax.ShapeDtypeStruct((mat.shape[0], width), BF16))
    out_cast_specs = [pl.BlockSpec(s.block_shape, lambda b, i: (b * (T // tm) + i, 0)) for s in cast_specs]
    outs = pl.pallas_call(
        functools.partial(_inproj_body, n_sub=max(tm // ROW_BLOCK, 1), n_cast=len(cast)),
        grid=(B, T // tm),
        in_specs=[
            pl.BlockSpec((1, tm, D), lambda b, i: (b, i, 0)),
            pl.BlockSpec((1, D), lambda b, i: (0, 0)),
            pl.BlockSpec(w.shape, lambda b, i: (0, 0)),
        ] + cast_specs,
        out_specs=[tok] * 4 + out_cast_specs,
        out_shape=[jax.ShapeDtypeStruct((B, T, NA_WIDTH), BF16)] * 3
        + [jax.ShapeDtypeStruct((B, T, FN_WIDTH), F32)] + cast_shapes,
        compiler_params=pltpu.CompilerParams(
            dimension_semantics=("arbitrary", "arbitrary"), vmem_limit_bytes=VMEM_LIMIT),
        name="inproj",
    )(x, g, w, *[mat for mat, _, _ in cast])
    return tuple(outs[:4]), tuple(outs[4:])


Q_ROWS = 8
Q_COLS = 16
K_ROWS = NA_WIN_ROWS + Q_ROWS - 1
K_COLS = 2 * NA_WIN_COLS
K_COL_START = (0, 8, 24, 32)
UNIT_KEYS = 512
UNIT_QUERIES = Q_ROWS * Q_COLS


def _attn_body(q_ref, k_ref, v_ref, km_ref, vm_ref, bias_ref, o_ref, *, rows, rows_per_step):
    chunk = pl.program_id(2)
    first_head = lax.broadcasted_iota(jnp.int32, (UNIT_QUERIES, LANES), 1) < NA_HEAD_DIM
    first_head_kv = lax.broadcasted_iota(jnp.int32, (UNIT_KEYS, LANES), 1) < NA_HEAD_DIM
    pad = jnp.zeros((UNIT_KEYS - K_ROWS * K_COLS - N_META, LANES), BF16)
    km = jnp.concatenate([km_ref[...], pad], axis=0)
    vm = jnp.concatenate([vm_ref[...], pad], axis=0)
    zero = jnp.zeros((UNIT_QUERIES, LANES), BF16)
    ones = jnp.ones((UNIT_KEYS, LANES), BF16)

    def window(ref, start_row, g):
        c0 = K_COL_START[g]
        pieces = []
        for w in range(K_ROWS):
            base = pl.multiple_of((start_row + w) * GRID_W, GRID_W)
            if c0 % 16 == 0:
                pieces.append(ref[0, pl.ds(base + c0, K_COLS), :])
            else:
                span = ref[0, pl.ds(base + c0 - SUBLANES, K_COLS + 16), :].astype(F32)
                pieces.append(span[SUBLANES:SUBLANES + K_COLS].astype(BF16))
        return pieces

    def scores(unit):
        blk, g = divmod(unit, GRID_W // Q_COLS)
        r0 = chunk * rows_per_step + Q_ROWS * blk
        start = jnp.clip(r0 - NA_WIN_ROWS // 2, 0, rows - K_ROWS)
        cfg = jnp.where(r0 == 0, 0, jnp.where(r0 == rows - Q_ROWS, 2, 1))
        q2 = jnp.concatenate(
            [q_ref[0, (Q_ROWS * blk + i) * GRID_W + Q_COLS * g:(Q_ROWS * blk + i) * GRID_W + Q_COLS * (g + 1), :]
             for i in range(Q_ROWS)], axis=0)
        qq = jnp.concatenate([jnp.where(first_head, q2, zero), jnp.where(first_head, zero, q2)], axis=0)
        kw = jnp.concatenate(window(k_ref, start, g) + [km], axis=0)
        s = lax.dot_general(qq, kw, (((1,), (1,)), ((), ())), preferred_element_type=F32)
        bias = jnp.concatenate([bias_ref[cfg, g, 0, 0], bias_ref[cfg, g, 0, 1]], axis=0)
        return s + bias, start

    def finish(unit, s, start):
        blk, g = divmod(unit, GRID_W // Q_COLS)
        vw = jnp.concatenate(window(v_ref, start, g) + [vm], axis=0)
        v0 = jnp.where(first_head_kv, vw, ones)
        v1 = jnp.where(first_head_kv, ones, vw)
        m = jnp.max(s, axis=-1, keepdims=True)
        p = jnp.exp2(s - m).astype(BF16)
        o0 = jnp.dot(p[:UNIT_QUERIES], v0, preferred_element_type=F32)
        o1 = jnp.dot(p[UNIT_QUERIES:], v1, preferred_element_type=F32)
        r0 = o0 / pltpu.roll(o0, NA_HEAD_DIM, axis=1)
        r1 = o1 / pltpu.roll(o1, NA_HEAD_DIM, axis=1)
        out = jnp.where(first_head, r0, r1).astype(BF16)
        for i in range(Q_ROWS):
            lo = (Q_ROWS * blk + i) * GRID_W + Q_COLS * g
            o_ref[0, lo:lo + Q_COLS, :] = out[i * Q_COLS:(i + 1) * Q_COLS]

    n_units = (rows_per_step // Q_ROWS) * (GRID_W // Q_COLS)
    queue = [scores(u) for u in range(min(SCORES_AHEAD, n_units))]
    for u in range(n_units):
        if u + SCORES_AHEAD < n_units:
            queue.append(scores(u + SCORES_AHEAD))
        finish(u, *queue.pop(0))


def _attention(q, k, v, km, vm, bias, rows_per_step):
    B, T, _ = q.shape
    rows = T // GRID_W
    assert T % GRID_W == 0 and rows % rows_per_step == 0 and rows_per_step % Q_ROWS == 0 and rows >= K_ROWS + Q_ROWS
    tq = rows_per_step * GRID_W
    body = functools.partial(_attn_body, rows=rows, rows_per_step=rows_per_step)
    return pl.pallas_call(
        body,
        grid=(B, HEAD_PAIRS, rows // rows_per_step),
        in_specs=[
            pl.BlockSpec((1, tq, LANES), lambda b, h, c: (b, c, h)),
            pl.BlockSpec((1, T, LANES), lambda b, h, c: (b, 0, h)),
            pl.BlockSpec((1, T, LANES), lambda b, h, c: (b, 0, h)),
            pl.BlockSpec((N_META, LANES), lambda b, h, c: (0, h)),
            pl.BlockSpec((N_META, LANES), lambda b, h, c: (0, h)),
            pl.BlockSpec(bias.shape[:2] + (1,) + bias.shape[3:], lambda b, h, c: (0, 0, h, 0, 0, 0)),
        ],
        out_specs=pl.BlockSpec((1, tq, LANES), lambda b, h, c: (b, c, h)),
        out_shape=jax.ShapeDtypeStruct((B, T, NA_WIDTH), BF16),
        compiler_params=pltpu.CompilerParams(
            dimension_semantics=("arbitrary", "arbitrary", "arbitrary"), vmem_limit_bytes=VMEM_LIMIT),
        name="attention",
    )(q, k, v, km, vm, bias)


UNIT_CONFIGS = ((0, tuple(max(i - 4, 0) for i in range(Q_ROWS))),
                (4, tuple(range(Q_ROWS))),
                (7, tuple(min(3 + i, 7) for i in range(Q_ROWS))))


def _bias_tables(rel_bias, meta_bias):
    n_groups = GRID_W // Q_COLS
    n_off = 2 * NA_WIN_COLS - 1
    n_row_off = 2 * NA_WIN_ROWS - 1
    hi = lax.Precision.HIGHEST
    onehot = np.zeros((n_groups, n_off, Q_COLS, K_COLS), np.float32)
    colmask = np.full((n_groups, Q_COLS, K_COLS), NEG, np.float32)
    for g in range(n_groups):
        for jq in range(Q_COLS):
            c = Q_COLS * g + jq
            cs = min(max(c - NA_WIN_COLS // 2, 0), GRID_W - NA_WIN_COLS)
            for x in range(K_COLS):
                j = K_COL_START[g] + x
                if cs <= j < cs + NA_WIN_COLS:
                    onehot[g, j - c + NA_WIN_COLS - 1, jq, x] = 1.0
                    colmask[g, jq, x] = 0.0
    blocks = jnp.einsum("hrd,gdjx->rghjx", rel_bias * LOG2E, onehot, precision=hi) + colmask[None, :, None]
    shape = (1, n_groups, NA_HEADS, Q_COLS, K_COLS)
    negblk = jnp.full(shape, NEG, F32)
    metablk = jnp.broadcast_to(jnp.concatenate(
        [meta_bias * LOG2E, jnp.full((NA_HEADS, K_COLS - N_META), NEG, F32)], axis=-1)[None, None, :, None, :], shape)
    blocks = jnp.concatenate([blocks, negblk, metablk], axis=0)
    slots = LANES // K_COLS
    placed = [jnp.pad(blocks, ((0, 0),) * 4 + ((K_COLS * kk, LANES - K_COLS * (kk + 1)),)) for kk in range(slots)]
    source = np.full((len(UNIT_CONFIGS), Q_ROWS, K_ROWS + 1), n_row_off, np.int32)
    for cc, (qslot, firsts) in enumerate(UNIT_CONFIGS):
        for i, w0 in enumerate(firsts):
            for w in range(w0, w0 + NA_WIN_ROWS):
                source[cc, i, w] = w - qslot - i + NA_WIN_ROWS - 1
            source[cc, i, K_ROWS] = n_row_off + 1

    def assemble(*refs):
        out_ref = refs[-1]
        for cc in range(source.shape[0]):
            for i in range(Q_ROWS):
                for vv in range((K_ROWS + 1) // slots):
                    acc = refs[0][int(source[cc, i, slots * vv])]
                    for kk in range(1, slots):
                        acc = acc + refs[kk][int(source[cc, i, slots * vv + kk])]
                    out_ref[cc, :, :, i * Q_COLS:(i + 1) * Q_COLS, vv * LANES:(vv + 1) * LANES] = acc

    t = pl.pallas_call(
        assemble,
        grid=(n_groups,),
        in_specs=[pl.BlockSpec((blocks.shape[0], 1) + blocks.shape[2:4] + (LANES,), lambda g: (0, g, 0, 0, 0))] * slots,
        out_specs=pl.BlockSpec((len(UNIT_CONFIGS), 1, NA_HEADS, UNIT_QUERIES, UNIT_KEYS), lambda g: (0, g, 0, 0, 0)),
        out_shape=jax.ShapeDtypeStruct((len(UNIT_CONFIGS), n_groups, NA_HEADS, UNIT_QUERIES, UNIT_KEYS), F32),
        compiler_params=pltpu.CompilerParams(dimension_semantics=("arbitrary",), vmem_limit_bytes=VMEM_LIMIT),
        name="bias_assemble",
    )(*placed)
    return t.reshape(t.shape[:2] + (HEAD_PAIRS, 2, UNIT_QUERIES, UNIT_KEYS))


def _snap(x):
    for v in (0.0, 1.0, -1.0):
        if abs(x - v) < 1e-12:
            return v
    return x


def _scaled(x, coef):
    if x is None or coef == 0.0:
        return None
    arr, sign = x
    coef = coef * sign
    if abs(coef) == 1.0:
        return arr, coef
    return arr * coef, 1.0


def _plus(a, b):
    if a is None:
        return b
    if b is None:
        return a
    (x, sx), (y, sy) = a, b
    if sx == sy:
        return x + y, sx
    return (x - y, 1.0) if sx > 0 else (y - x, 1.0)


def _negated(a):
    return None if a is None else (a[0], -a[1])


def _value(a):
    if a is None:
        return None
    return a[0] if a[1] > 0 else -a[0]


def _fft_real(xs):
    n = len(xs)
    if n == 1:
        return [(xs[0], None)]
    even, odd = _fft_real(xs[0::2]), _fft_real(xs[1::2])

    def bin_of(half, kk):
        if kk <= n // 4:
            return half[kk]
        re, im = half[n // 2 - kk]
        return re, _negated(im)

    out = []
    for kk in range(n // 2 + 1):
        c, s = _snap(math.cos(2 * math.pi * kk / n)), _snap(math.sin(2 * math.pi * kk / n))
        er, ei = bin_of(even, kk % (n // 2))
        orr, oi = bin_of(odd, kk % (n // 2))
        tr = _plus(_scaled(orr, c), _scaled(oi, s))
        ti = _plus(_scaled(oi, c), _scaled(orr, -s))
        out.append((_plus(er, tr), _plus(ei, ti)))
    return out


def _dft16_real(xs):
    return [(_value(re), _value(im)) for re, im in _fft_real([(x, 1.0) for x in xs])]


def _fourier_body(u_ref, um_ref, g_ref, gt_ref, twc_ref, tws_ref, cs_ref, f_ref,
                  zr_ref, zi_ref, sh_ref, *, n2):
    nm = n2 - 1
    half = DFT_OUTER // 2
    def stage1(r0, meta_rows):
        xs = [um_ref[pl.ds(r0, SUBLANES), :] if (a == 0 and meta_rows)
              else u_ref[0, pl.ds(n2 * a - N_META + r0, SUBLANES), :] for a in range(DFT_OUTER)]
        for c, (zr, zi) in enumerate(_dft16_real(xs)):
            zr_ref[c, pl.ds(r0, SUBLANES), :] = zr
            if zi is not None:
                zi_ref[c, pl.ds(r0, SUBLANES), :] = zi

    for i in range(N_META // SUBLANES):
        stage1(i * SUBLANES, True)

    def stage1_step(i, carry):
        stage1(pl.multiple_of(i * SUBLANES, SUBLANES), False)
        return carry

    lax.fori_loop(N_META // SUBLANES, nm // SUBLANES, stage1_step, 0, unroll=2)
    zt = _dft16_real([u_ref[0, pl.ds(n2 * a + nm - N_META, 1), :] for a in range(DFT_OUTER)])

    g = g_ref[...]
    gt = gt_ref[...]
    csm = cs_ref[...]
    inv_sqrt_n = 1.0 / math.sqrt(DFT_OUTER * n2)

    def prep(s):
        zr = zr_ref[s]
        ztr, zti = zt[s]
        zi = None if s in (0, half) else zi_ref[s]
        if s > 0:
            tc, ts = twc_ref[0:nm, s:s + 1], tws_ref[0:nm, s:s + 1]
            tct, tst = twc_ref[nm:n2, s:s + 1], tws_ref[nm:n2, s:s + 1]
            if zi is None:
                zr, zi = zr * tc, -(zr * ts)
                ztr, zti = ztr * tct, -(ztr * tst)
            else:
                zr, zi = zr * tc + zi * ts, zi * tc - zr * ts
                ztr, zti = ztr * tct + zti * tst, zti * tct - ztr * tst
        if zi is None:
            zi, zti = jnp.zeros_like(zr), jnp.zeros_like(ztr)
        z = jnp.concatenate([zr, zi], axis=1)
        tail = jnp.concatenate([ztr, zti], axis=1)
        return z, tail

    def dense(z, tail):
        p = jnp.dot(g, z.astype(BF16), preferred_element_type=F32)
        return p + gt * tail

    def channel_store(c, pr, pi):
        y = jnp.dot(jnp.concatenate([pr, pi], axis=1).astype(BF16), csm, preferred_element_type=F32)
        f_ref[0, pl.ds(c, nm, stride=DFT_OUTER), :] = y

    def post(s, p, z, tail):
        czr, czi, szr, szi = p[:nm, :LANES], p[:nm, LANES:], p[nm:, :LANES], p[nm:, LANES:]
        channel_store(s, czr + szi, czi - szr)
        if 0 < s < half:
            col = (jnp.sum(z, axis=0, keepdims=True) + tail) * inv_sqrt_n
            sh_ref[0, 0:nm] = czr - szi
            sh_ref[1, 0:nm] = -(czi + szr)
            sh_ref[0, nm:n2] = col[:, :LANES]
            sh_ref[1, nm:n2] = -col[:, LANES:]
            channel_store(DFT_OUTER - s, sh_ref[0, 1:n2], sh_ref[1, 1:n2])

    pending = None
    for s in range(half + 1):
        z, tail = prep(s)
        p = dense(z, tail)
        if pending is not None:
            post(*pending)
        pending = (s, p, z, tail)
    post(*pending)


def _fourier(u, um, g, gt, twc, tws, cs):
    B, T, _ = u.shape
    assert N_META == DFT_OUTER and T % (2 * SUBLANES * DFT_OUTER) == 0
    n2 = (N_META + T) // DFT_OUTER
    nm = n2 - 1
    body = functools.partial(_fourier_body, n2=n2)

    def const(arr):
        return pl.BlockSpec(arr.shape, lambda b, j: (0, 0))

    return pl.pallas_call(
        body,
        grid=(B, FN_GROUPS),
        in_specs=[
            pl.BlockSpec((1, T, LANES), lambda b, j: (b, 0, j)),
            pl.BlockSpec((N_META, LANES), lambda b, j: (0, j)),
            const(g), const(gt), const(twc), const(tws), const(cs),
        ],
        out_specs=pl.BlockSpec((1, T, LANES), lambda b, j: (b, 0, j)),
        out_shape=jax.ShapeDtypeStruct((B, T, FN_WIDTH), F32),
        scratch_shapes=[
            pltpu.VMEM((DFT_OUTER // 2 + 1, nm, LANES), F32),
            pltpu.VMEM((DFT_OUTER // 2 + 1, nm, LANES), F32),
            pltpu.VMEM((2, nm + SUBLANES, LANES), F32),
        ],
        compiler_params=pltpu.CompilerParams(
            dimension_semantics=("arbitrary", "arbitrary"), vmem_limit_bytes=VMEM_LIMIT),
        name="fourier",
    )(u, um, g, gt, twc, tws, cs)


def _fourier_constants(n2):
    n = DFT_OUTER * n2
    b = jnp.arange(n2, dtype=jnp.int32)
    d = jnp.arange(1, n2, dtype=jnp.int32)
    th = ((d[:, None] * b[None, :]) % n2).astype(F32) * (2.0 * math.pi / n2)
    full = jnp.concatenate([jnp.cos(th), jnp.sin(th)], axis=0) * (1.0 / math.sqrt(n))
    g, gt = full[:, :n2 - 1].astype(BF16), full[:, n2 - 1:]
    c = jnp.arange(DFT_OUTER, dtype=jnp.int32)
    tw = ((b[:, None] * c[None, :]) % n).astype(F32) * (2.0 * math.pi / n)
    return g, gt, jnp.cos(tw), jnp.sin(tw)


def _channel_dft_matrix():
    ch = jnp.arange(FN_GROUP_DIM, dtype=jnp.int32)
    th = ((ch[:, None] * ch[None, :]) % FN_GROUP_DIM).astype(F32) * (2.0 * math.pi / FN_GROUP_DIM)
    scale = 1.0 / math.sqrt(FN_GROUP_DIM)
    return jnp.concatenate([jnp.cos(th) * scale, jnp.sin(th) * scale], axis=0).astype(BF16)


def _trunk_body(x_ref, o_ref, f_ref, gmix_ref, wg_ref, wna_ref, wfn_ref, wout_ref, gmlp_ref,
                wup_ref, wdn_ref, gfin_ref, y_ref, *, ff_chunk, n_sub):
    def dot(a, b):
        return jnp.dot(a, b, preferred_element_type=F32)

    sub = x_ref.shape[1] // n_sub
    n_ff = D_FF // ff_chunk

    def chain(r):
        rows = slice(r * sub, (r + 1) * sub)
        x = x_ref[0, rows, :]
        h = _rms(x, gmix_ref[...]).astype(BF16)
        gates = dot(h, wg_ref[...])
        y_na = dot(o_ref[0, rows, :], wna_ref[...])
        y_fn = dot(f_ref[0, rows, :].astype(BF16), wfn_ref[...])
        yield
        mixed = jax.nn.sigmoid(gates[:, :D_MODEL]) * y_na + jax.nn.sigmoid(gates[:, D_MODEL:]) * y_fn
        x1 = x + dot(mixed.astype(BF16), wout_ref[...])
        yield
        h2 = _rms(x1, gmlp_ref[...]).astype(BF16)
        acc = x1
        up = dot(h2, wup_ref[:, 0:ff_chunk])
        yield
        for c in range(n_ff):
            a = jnp.maximum(up, 0.0)
            a = (a * a).astype(BF16)
            if c + 1 < n_ff:
                up = dot(h2, wup_ref[:, (c + 1) * ff_chunk:(c + 2) * ff_chunk])
            acc = acc + dot(a, wdn_ref[c * ff_chunk:(c + 1) * ff_chunk, :])
            yield
        y_ref[0, rows, :] = _rms(acc, gfin_ref[...])

    live = [chain(r) for r in range(n_sub)]
    while live:
        for g in list(live):
            try:
                next(g)
            except StopIteration:
                live.remove(g)


def _trunk(x, o, f, gmix, wg, wna, wfn, wout, gmlp, wup, wdn, gfin, tm):
    B, T, D = x.shape
    assert T % tm == 0 and tm % ROW_BLOCK == 0 and D_FF % FF_CHUNK == 0
    body = functools.partial(_trunk_body, ff_chunk=FF_CHUNK, n_sub=tm // ROW_BLOCK)

    def const(arr):
        return pl.BlockSpec(arr.shape, lambda b, i: (0, 0), pipeline_mode=pl.Buffered(1))

    return pl.pallas_call(
        body,
        grid=(B, T // tm),
        in_specs=[
            pl.BlockSpec((1, tm, D), lambda b, i: (b, i, 0)),
            pl.BlockSpec((1, tm, NA_WIDTH), lambda b, i: (b, i, 0)),
            pl.BlockSpec((1, tm, FN_WIDTH), lambda b, i: (b, i, 0)),
            const(gmix), const(wg), const(wna), const(wfn), const(wout), const(gmlp),
            const(wup), const(wdn), const(gfin),
        ],
        out_specs=pl.BlockSpec((1, tm, D), lambda b, i: (b, i, 0)),
        out_shape=jax.ShapeDtypeStruct((B, T, D), F32),
        compiler_params=pltpu.CompilerParams(
            dimension_semantics=("arbitrary", "arbitrary"), vmem_limit_bytes=TRUNK_VMEM_LIMIT),
        name="trunk",
    )(x, o, f, gmix, wg, wna, wfn, wout, gmlp, wup, wdn, gfin)


def _run_group(x, qkvu, meta, consts):
    (cs, bias, g_mix, wg, wna, wfn, wout, g_mlp, wup, wdn, g_fin) = consts
    B, T, _ = x.shape
    n2 = (N_META + T) // DFT_OUTER
    q, k, v, u = qkvu
    _, km, vm, um = meta
    o = _attention(q, k, v, km, vm, bias, rows_per_step=min(T // GRID_W, ATTN_ROWS_PER_STEP))
    f = _fourier(u, um, *_fourier_constants(n2), cs)
    return _trunk(x, o, f, g_mix, wg, wna, wfn, wout, g_mlp, wup, wdn, g_fin, tm=TRUNK_TOKENS)


def kernel(x_prompt, x_sample, meta_tokens, w_in, rel_bias, meta_bias, w_branch_na, w_branch_fn,
           w_out, g_mix, g_mlp, w_up, w_down, g_final):
    assert w_in.shape[0] == 1, "single-layer trunk"
    w_in0 = w_in[0]
    split = 3 * NA_WIDTH + FN_WIDTH
    w_qkvu = w_in0[:, :split].astype(BF16)
    cs = _channel_dft_matrix()
    bias = _bias_tables(rel_bias[0], meta_bias[0])
    g_mix2 = g_mix[0][None]
    meta, _ = _inproj(meta_tokens[None], g_mix2, w_qkvu, tm=N_META)
    meta = tuple(m[0] for m in meta)
    later = [(w_in0, 2 * D_MODEL, split // (2 * D_MODEL)), (w_branch_na[0], D_MODEL, 0), (w_branch_fn[0], D_MODEL, 0),
             (w_out[0], D_MODEL, 0), (w_up[0], D_FF, 0), (w_down[0], D_MODEL, 0)]
    qkvu_prompt, (wg, wna, wfn, wout, wup, wdn) = _inproj(x_prompt, g_mix2, w_qkvu, tm=INPROJ_TOKENS, cast=later)
    qkvu_sample, _ = _inproj(x_sample, g_mix2, w_qkvu, tm=INPROJ_TOKENS)
    consts = (cs, bias, g_mix2, wg, wna, wfn, wout, g_mlp[0][None], wup, wdn, g_final[None])
    return (_run_group(x_prompt, qkvu_prompt, meta, consts), _run_group(x_sample, qkvu_sample, meta, consts))
```

```python
import functools
import math

import jax
import jax.numpy as jnp
import numpy as np
from jax import lax
from jax.experimental import pallas as pl
from jax.experimental.pallas import tpu as pltpu

F32 = jnp.float32
BF16 = jnp.bfloat16

D_MODEL = 1024
N_META = 16
GRID_W = 64
NA_HEADS = 8
NA_HEAD_DIM = 64
NA_WIDTH = NA_HEADS * NA_HEAD_DIM
NA_WIN_ROWS = 8
NA_WIN_COLS = 16
FN_GROUPS = 4
FN_GROUP_DIM = 128
FN_WIDTH = FN_GROUPS * FN_GROUP_DIM
D_FF = 4 * D_MODEL
RMS_EPS = 1e-6

LANES = 128
SUBLANES = 8
HEAD_PAIRS = NA_HEADS // 2
SCORES_AHEAD = 2
ROW_BLOCK = 256
NEG = -1e30
LOG2E = math.log2(math.e)
DFT_OUTER = 16
VMEM_LIMIT = 56 * 1024 * 1024
INPROJ_TOKENS = 8 * ROW_BLOCK
TRUNK_TOKENS = 4 * ROW_BLOCK
ATTN_ROWS_PER_STEP = 128
FF_CHUNK = 512
TRUNK_VMEM_LIMIT = 60 * 1024 * 1024


def _rms(x, g):
    ms = jnp.mean(x * x, axis=-1, keepdims=True)
    return x * lax.rsqrt(ms + RMS_EPS) * g


def _inproj_body(x_ref, g_ref, w_ref, *refs, n_sub, n_cast):
    cast_in, (q_ref, k_ref, v_ref, u_ref), cast_out = refs[:n_cast], refs[n_cast:n_cast + 4], refs[n_cast + 4:]
    sub = x_ref.shape[1] // n_sub
    for r in range(n_sub):
        rows = slice(r * sub, (r + 1) * sub)
        h = _rms(x_ref[0, rows, :], g_ref[...]).astype(BF16)
        z = jnp.dot(h, w_ref[...], preferred_element_type=F32)
        q_ref[0, rows, :] = (z[:, :NA_WIDTH] * (NA_HEAD_DIM ** -0.5 * LOG2E)).astype(BF16)
        k_ref[0, rows, :] = z[:, NA_WIDTH:2 * NA_WIDTH].astype(BF16)
        v_ref[0, rows, :] = z[:, 2 * NA_WIDTH:3 * NA_WIDTH].astype(BF16)
        u_ref[0, rows, :] = z[:, 3 * NA_WIDTH:]
    for src, dst in zip(cast_in, cast_out):
        dst[...] = src[...].astype(BF16)


def _inproj(x, g, w, tm, cast=()):
    B, T, D = x.shape
    assert T % tm == 0 and (tm % ROW_BLOCK == 0 or tm < ROW_BLOCK)
    steps = B * (T // tm)
    tok = pl.BlockSpec((1, tm, NA_WIDTH), lambda b, i: (b, i, 0))
    cast_specs, cast_shapes = [], []
    for mat, width, col in cast:
        rows = mat.shape[0] // steps
        assert mat.shape[0] % steps == 0 and rows % 16 == 0 and width % LANES == 0
        cast_specs.append(pl.BlockSpec((rows, width), lambda b, i, col=col: (b * (T // tm) + i, col)))
        cast_shapes.append(jax.ShapeDtypeStruct((mat.shape[0], width), BF16))
    out_cast_specs = [pl.BlockSpec(s.block_shape, lambda b, i: (b * (T // tm) + i, 0)) for s in cast_specs]
    outs = pl.pallas_call(
        functools.partial(_inproj_body, n_sub=max(tm // ROW_BLOCK, 1), n_cast=len(cast)),
        grid=(B, T // tm),
        in_specs=[
            pl.BlockSpec((1, tm, D), lambda b, i: (b, i, 0)),
            pl.BlockSpec((1, D), lambda b, i: (0, 0)),
            pl.BlockSpec(w.shape, lambda b, i: (0, 0)),
        ] + cast_specs,
        out_specs=[tok] * 4 + out_cast_specs,
        out_shape=[jax.ShapeDtypeStruct((B, T, NA_WIDTH), BF16)] * 3
        + [jax.ShapeDtypeStruct((B, T, FN_WIDTH), F32)] + cast_shapes,
        compiler_params=pltpu.CompilerParams(
            dimension_semantics=("arbitrary", "arbitrary"), vmem_limit_bytes=VMEM_LIMIT),
        name="inproj",
    )(x, g, w, *[mat for mat, _, _ in cast])
    return tuple(outs[:4]), tuple(outs[4:])


Q_ROWS = 8
Q_COLS = 16
K_ROWS = NA_WIN_ROWS + Q_ROWS - 1
K_COLS = 2 * NA_WIN_COLS
K_COL_START = (0, 8, 24, 32)
UNIT_KEYS = 512
UNIT_QUERIES = Q_ROWS * Q_COLS


def _attn_body(q_ref, k_ref, v_ref, km_ref, vm_ref, bias_ref, o_ref, *, rows, rows_per_step):
    chunk = pl.program_id(2)
    first_head = lax.broadcasted_iota(jnp.int32, (UNIT_QUERIES, LANES), 1) < NA_HEAD_DIM
    first_head_kv = lax.broadcasted_iota(jnp.int32, (UNIT_KEYS, LANES), 1) < NA_HEAD_DIM
    pad = jnp.zeros((UNIT_KEYS - K_ROWS * K_COLS - N_META, LANES), BF16)
    km = jnp.concatenate([km_ref[...], pad], axis=0)
    vm = jnp.concatenate([vm_ref[...], pad], axis=0)
    zero = jnp.zeros((UNIT_QUERIES, LANES), BF16)
    ones = jnp.ones((UNIT_KEYS, LANES), BF16)

    def window(ref, start_row, g):
        c0 = K_COL_START[g]
        pieces = []
        for w in range(K_ROWS):
            base = pl.multiple_of((start_row + w) * GRID_W, GRID_W)
            if c0 % 16 == 0:
                pieces.append(ref[0, pl.ds(base + c0, K_COLS), :])
            else:
                span = ref[0, pl.ds(base + c0 - SUBLANES, K_COLS + 16), :].astype(F32)
                pieces.append(span[SUBLANES:SUBLANES + K_COLS].astype(BF16))
        return pieces

    def scores(unit):
        blk, g = divmod(unit, GRID_W // Q_COLS)
        r0 = chunk * rows_per_step + Q_ROWS * blk
        start = jnp.clip(r0 - NA_WIN_ROWS // 2, 0, rows - K_ROWS)
        cfg = jnp.where(r0 == 0, 0, jnp.where(r0 == rows - Q_ROWS, 2, 1))
        q2 = jnp.concatenate(
            [q_ref[0, (Q_ROWS * blk + i) * GRID_W + Q_COLS * g:(Q_ROWS * blk + i) * GRID_W + Q_COLS * (g + 1), :]
             for i in range(Q_ROWS)], axis=0)
        qq = jnp.concatenate([jnp.where(first_head, q2, zero), jnp.where(first_head, zero, q2)], axis=0)
        kw = jnp.concatenate(window(k_ref, start, g) + [km], axis=0)
        s = lax.dot_general(qq, kw, (((1,), (1,)), ((), ())), preferred_element_type=F32)
        bias = jnp.concatenate([bias_ref[cfg, g, 0, 0], bias_ref[cfg, g, 0, 1]], axis=0)
        return s + bias, start

    def finish(unit, s, start):
        blk, g = divmod(unit, GRID_W // Q_COLS)
        vw = jnp.concatenate(window(v_ref, start, g) + [vm], axis=0)
        v0 = jnp.where(first_head_kv, vw, ones)
        v1 = jnp.where(first_head_kv, ones, vw)
        m = jnp.max(s, axis=-1, keepdims=True)
        p = jnp.exp2(s - m).astype(BF16)
        o0 = jnp.dot(p[:UNIT_QUERIES], v0, preferred_element_type=F32)
        o1 = jnp.dot(p[UNIT_QUERIES:], v1, preferred_element_type=F32)
        r0 = o0 / pltpu.roll(o0, NA_HEAD_DIM, axis=1)
        r1 = o1 / pltpu.roll(o1, NA_HEAD_DIM, axis=1)
        out = jnp.where(first_head, r0, r1).astype(BF16)
        for i in range(Q_ROWS):
            lo = (Q_ROWS * blk + i) * GRID_W + Q_COLS * g
            o_ref[0, lo:lo + Q_COLS, :] = out[i * Q_COLS:(i + 1) * Q_COLS]

    n_units = (rows_per_step // Q_ROWS) * (GRID_W // Q_COLS)
    queue = [scores(u) for u in range(min(SCORES_AHEAD, n_units))]
    for u in range(n_units):
        if u + SCORES_AHEAD < n_units:
            queue.append(scores(u + SCORES_AHEAD))
        finish(u, *queue.pop(0))


def _attention(q, k, v, km, vm, bias, rows_per_step):
    B, T, _ = q.shape
    rows = T // GRID_W
    assert T % GRID_W == 0 and rows % rows_per_step == 0 and rows_per_step % Q_ROWS == 0 and rows >= K_ROWS + Q_ROWS
    tq = rows_per_step * GRID_W
    body = functools.partial(_attn_body, rows=rows, rows_per_step=rows_per_step)
    return pl.pallas_call(
        body,
        grid=(B, HEAD_PAIRS, rows // rows_per_step),
        in_specs=[
            pl.BlockSpec((1, tq, LANES), lambda b, h, c: (b, c, h)),
            pl.BlockSpec((1, T, LANES), lambda b, h, c: (b, 0, h)),
            pl.BlockSpec((1, T, LANES), lambda b, h, c: (b, 0, h)),
            pl.BlockSpec((N_META, LANES), lambda b, h, c: (0, h)),
            pl.BlockSpec((N_META, LANES), lambda b, h, c: (0, h)),
            pl.BlockSpec(bias.shape[:2] + (1,) + bias.shape[3:], lambda b, h, c: (0, 0, h, 0, 0, 0)),
        ],
        out_specs=pl.BlockSpec((1, tq, LANES), lambda b, h, c: (b, c, h)),
        out_shape=jax.ShapeDtypeStruct((B, T, NA_WIDTH), BF16),
        compiler_params=pltpu.CompilerParams(
            dimension_semantics=("arbitrary", "arbitrary", "arbitrary"), vmem_limit_bytes=VMEM_LIMIT),
        name="attention",
    )(q, k, v, km, vm, bias)


UNIT_CONFIGS = ((0, tuple(max(i - 4, 0) for i in range(Q_ROWS))),
                (4, tuple(range(Q_ROWS))),
                (7, tuple(min(3 + i, 7) for i in range(Q_ROWS))))


def _bias_tables(rel_bias, meta_bias):
    n_groups = GRID_W // Q_COLS
    n_off = 2 * NA_WIN_COLS - 1
    n_row_off = 2 * NA_WIN_ROWS - 1
    hi = lax.Precision.HIGHEST
    onehot = np.zeros((n_groups, n_off, Q_COLS, K_COLS), np.float32)
    colmask = np.full((n_groups, Q_COLS, K_COLS), NEG, np.float32)
    for g in range(n_groups):
        for jq in range(Q_COLS):
            c = Q_COLS * g + jq
            cs = min(max(c - NA_WIN_COLS // 2, 0), GRID_W - NA_WIN_COLS)
            for x in range(K_COLS):
                j = K_COL_START[g] + x
                if cs <= j < cs + NA_WIN_COLS:
                    onehot[g, j - c + NA_WIN_COLS - 1, jq, x] = 1.0
                    colmask[g, jq, x] = 0.0
    blocks = jnp.einsum("hrd,gdjx->rghjx", rel_bias * LOG2E, onehot, precision=hi) + colmask[None, :, None]
    shape = (1, n_groups, NA_HEADS, Q_COLS, K_COLS)
    negblk = jnp.full(shape, NEG, F32)
    metablk = jnp.broadcast_to(jnp.concatenate(
        [meta_bias * LOG2E, jnp.full((NA_HEADS, K_COLS - N_META), NEG, F32)], axis=-1)[None, None, :, None, :], shape)
    blocks = jnp.concatenate([blocks, negblk, metablk], axis=0)
    slots = LANES // K_COLS
    placed = [jnp.pad(blocks, ((0, 0),) * 4 + ((K_COLS * kk, LANES - K_COLS * (kk + 1)),)) for kk in range(slots)]
    source = np.full((len(UNIT_CONFIGS), Q_ROWS, K_ROWS + 1), n_row_off, np.int32)
    for cc, (qslot, firsts) in enumerate(UNIT_CONFIGS):
        for i, w0 in enumerate(firsts):
            for w in range(w0, w0 + NA_WIN_ROWS):
                source[cc, i, w] = w - qslot - i + NA_WIN_ROWS - 1
            source[cc, i, K_ROWS] = n_row_off + 1

    def assemble(*refs):
        out_ref = refs[-1]
        for cc in range(source.shape[0]):
            for i in range(Q_ROWS):
                for vv in range((K_ROWS + 1) // slots):
                    acc = refs[0][int(source[cc, i, slots * vv])]
                    for kk in range(1, slots):
                        acc = acc + refs[kk][int(source[cc, i, slots * vv + kk])]
                    out_ref[cc, :, :, i * Q_COLS:(i + 1) * Q_COLS, vv * LANES:(vv + 1) * LANES] = acc

    t = pl.pallas_call(
        assemble,
        grid=(n_groups,),
        in_specs=[pl.BlockSpec((blocks.shape[0], 1) + blocks.shape[2:4] + (LANES,), lambda g: (0, g, 0, 0, 0))] * slots,
        out_specs=pl.BlockSpec((len(UNIT_CONFIGS), 1, NA_HEADS, UNIT_QUERIES, UNIT_KEYS), lambda g: (0, g, 0, 0, 0)),
        out_shape=jax.ShapeDtypeStruct((len(UNIT_CONFIGS), n_groups, NA_HEADS, UNIT_QUERIES, UNIT_KEYS), F32),
        compiler_params=pltpu.CompilerParams(dimension_semantics=("arbitrary",), vmem_limit_bytes=VMEM_LIMIT),
        name="bias_assemble",
    )(*placed)
    return t.reshape(t.shape[:2] + (HEAD_PAIRS, 2, UNIT_QUERIES, UNIT_KEYS))


def _snap(x):
    for v in (0.0, 1.0, -1.0):
        if abs(x - v) < 1e-12:
            return v
    return x


def _scaled(x, coef):
    if x is None or coef == 0.0:
        return None
    arr, sign = x
    coef = coef * sign
    if abs(coef) == 1.0:
        return arr, coef
    return arr * coef, 1.0


def _plus(a, b):
    if a is None:
        return b
    if b is None:
        return a
    (x, sx), (y, sy) = a, b
    if sx == sy:
        return x + y, sx
    return (x - y, 1.0) if sx > 0 else (y - x, 1.0)


def _negated(a):
    return None if a is None else (a[0], -a[1])


def _value(a):
    if a is None:
        return None
    return a[0] if a[1] > 0 else -a[0]


def _fft_real(xs):
    n = len(xs)
    if n == 1:
        return [(xs[0], None)]
    even, odd = _fft_real(xs[0::2]), _fft_real(xs[1::2])

    def bin_of(half, kk):
        if kk <= n // 4:
            return half[kk]
        re, im = half[n // 2 - kk]
        return re, _negated(im)

    out = []
    for kk in range(n // 2 + 1):
        c, s = _snap(math.cos(2 * math.pi * kk / n)), _snap(math.sin(2 * math.pi * kk / n))
        er, ei = bin_of(even, kk % (n // 2))
        orr, oi = bin_of(odd, kk % (n // 2))
        tr = _plus(_scaled(orr, c), _scaled(oi, s))
        ti = _plus(_scaled(oi, c), _scaled(orr, -s))
        out.append((_plus(er, tr), _plus(ei, ti)))
    return out


def _dft16_real(xs):
    return [(_value(re), _value(im)) for re, im in _fft_real([(x, 1.0) for x in xs])]


def _fourier_body(u_ref, um_ref, g_ref, gt_ref, twc_ref, tws_ref, cs_ref, f_ref,
                  zr_ref, zi_ref, sh_ref, *, n2):
    nm = n2 - 1
    half = DFT_OUTER // 2
    def stage1(r0, meta_rows):
        xs = [um_ref[pl.ds(r0, SUBLANES), :] if (a == 0 and meta_rows)
              else u_ref[0, pl.ds(n2 * a - N_META + r0, SUBLANES), :] for a in range(DFT_OUTER)]
        for c, (zr, zi) in enumerate(_dft16_real(xs)):
            zr_ref[c, pl.ds(r0, SUBLANES), :] = zr
            if zi is not None:
                zi_ref[c, pl.ds(r0, SUBLANES), :] = zi

    for i in range(N_META // SUBLANES):
        stage1(i * SUBLANES, True)

    def stage1_step(i, carry):
        stage1(pl.multiple_of(i * SUBLANES, SUBLANES), False)
        return carry

    lax.fori_loop(N_META // SUBLANES, nm // SUBLANES, stage1_step, 0, unroll=2)
    zt = _dft16_real([u_ref[0, pl.ds(n2 * a + nm - N_META, 1), :] for a in range(DFT_OUTER)])

    g = g_ref[...]
    gt = gt_ref[...]
    csm = cs_ref[...]
    inv_sqrt_n = 1.0 / math.sqrt(DFT_OUTER * n2)

    def prep(s):
        zr = zr_ref[s]
        ztr, zti = zt[s]
        zi = None if s in (0, half) else zi_ref[s]
        if s > 0:
            tc, ts = twc_ref[0:nm, s:s + 1], tws_ref[0:nm, s:s + 1]
            tct, tst = twc_ref[nm:n2, s:s + 1], tws_ref[nm:n2, s:s + 1]
            if zi is None:
                zr, zi = zr * tc, -(zr * ts)
                ztr, zti = ztr * tct, -(ztr * tst)
            else:
                zr, zi = zr * tc + zi * ts, zi * tc - zr * ts
                ztr, zti = ztr * tct + zti * tst, zti * tct - ztr * tst
        if zi is None:
            zi, zti = jnp.zeros_like(zr), jnp.zeros_like(ztr)
        z = jnp.concatenate([zr, zi], axis=1)
        tail = jnp.concatenate([ztr, zti], axis=1)
        return z, tail

    def dense(z, tail):
        p = jnp.dot(g, z.astype(BF16), preferred_element_type=F32)
        return p + gt * tail

    def channel_store(c, pr, pi):
        y = jnp.dot(jnp.concatenate([pr, pi], axis=1).astype(BF16), csm, preferred_element_type=F32)
        f_ref[0, pl.ds(c, nm, stride=DFT_OUTER), :] = y

    def post(s, p, z, tail):
        czr, czi, szr, szi = p[:nm, :LANES], p[:nm, LANES:], p[nm:, :LANES], p[nm:, LANES:]
        channel_store(s, czr + szi, czi - szr)
        if 0 < s < half:
            col = (jnp.sum(z, axis=0, keepdims=True) + tail) * inv_sqrt_n
            sh_ref[0, 0:nm] = czr - szi
            sh_ref[1, 0:nm] = -(czi + szr)
            sh_ref[0, nm:n2] = col[:, :LANES]
            sh_ref[1, nm:n2] = -col[:, LANES:]
            channel_store(DFT_OUTER - s, sh_ref[0, 1:n2], sh_ref[1, 1:n2])

    pending = None
    for s in range(half + 1):
        z, tail = prep(s)
        p = dense(z, tail)
        if pending is not None:
            post(*pending)
        pending = (s, p, z, tail)
    post(*pending)


def _fourier(u, um, g, gt, twc, tws, cs):
    B, T, _ = u.shape
    assert N_META == DFT_OUTER and T % (2 * SUBLANES * DFT_OUTER) == 0
    n2 = (N_META + T) // DFT_OUTER
    nm = n2 - 1
    body = functools.partial(_fourier_body, n2=n2)

    def const(arr):
        return pl.BlockSpec(arr.shape, lambda b, j: (0, 0))

    return pl.pallas_call(
        body,
        grid=(B, FN_GROUPS),
        in_specs=[
            pl.BlockSpec((1, T, LANES), lambda b, j: (b, 0, j)),
            pl.BlockSpec((N_META, LANES), lambda b, j: (0, j)),
            const(g), const(gt), const(twc), const(tws), const(cs),
        ],
        out_specs=pl.BlockSpec((1, T, LANES), lambda b, j: (b, 0, j)),
        out_shape=jax.ShapeDtypeStruct((B, T, FN_WIDTH), F32),
        scratch_shapes=[
            pltpu.VMEM((DFT_OUTER // 2 + 1, nm, LANES), F32),
            pltpu.VMEM((DFT_OUTER // 2 + 1, nm, LANES), F32),
            pltpu.VMEM((2, nm + SUBLANES, LANES), F32),
        ],
        compiler_params=pltpu.CompilerParams(
            dimension_semantics=("arbitrary", "arbitrary"), vmem_limit_bytes=VMEM_LIMIT),
        name="fourier",
    )(u, um, g, gt, twc, tws, cs)


def _fourier_constants(n2):
    n = DFT_OUTER * n2
    b = jnp.arange(n2, dtype=jnp.int32)
    d = jnp.arange(1, n2, dtype=jnp.int32)
    th = ((d[:, None] * b[None, :]) % n2).astype(F32) * (2.0 * math.pi / n2)
    full = jnp.concatenate([jnp.cos(th), jnp.sin(th)], axis=0) * (1.0 / math.sqrt(n))
    g, gt = full[:, :n2 - 1].astype(BF16), full[:, n2 - 1:]
    c = jnp.arange(DFT_OUTER, dtype=jnp.int32)
    tw = ((b[:, None] * c[None, :]) % n).astype(F32) * (2.0 * math.pi / n)
    return g, gt, jnp.cos(tw), jnp.sin(tw)


def _channel_dft_matrix():
    ch = jnp.arange(FN_GROUP_DIM, dtype=jnp.int32)
    th = ((ch[:, None] * ch[None, :]) % FN_GROUP_DIM).astype(F32) * (2.0 * math.pi / FN_GROUP_DIM)
    scale = 1.0 / math.sqrt(FN_GROUP_DIM)
    return jnp.concatenate([jnp.cos(th) * scale, jnp.sin(th) * scale], axis=0).astype(BF16)


def _trunk_body(x_ref, o_ref, f_ref, gmix_ref, wg_ref, wna_ref, wfn_ref, wout_ref, gmlp_ref,
                wup_hbm, wdn_hbm, gfin_ref, y_ref, wup_ref, wdn_ref, sem, *, ff_chunk, n_sub):
    def dot(a, b):
        return jnp.dot(a, b, preferred_element_type=F32)

    first = (pl.program_id(0) == 0) & (pl.program_id(1) == 0)

    def mlp_weight_copies():
        return (pltpu.make_async_copy(wup_hbm, wup_ref, sem.at[0]),
                pltpu.make_async_copy(wdn_hbm, wdn_ref, sem.at[1]))

    @pl.when(first)
    def _():
        for copy in mlp_weight_copies():
            copy.start()

    sub = x_ref.shape[1] // n_sub
    n_ff = D_FF // ff_chunk

    def chain(r):
        rows = slice(r * sub, (r + 1) * sub)
        x = x_ref[0, rows, :]
        h = _rms(x, gmix_ref[...]).astype(BF16)
        gates = dot(h, wg_ref[...])
        y_na = dot(o_ref[0, rows, :], wna_ref[...])
        y_fn = dot(f_ref[0, rows, :].astype(BF16), wfn_ref[...])
        yield
        mixed = jax.nn.sigmoid(gates[:, :D_MODEL]) * y_na + jax.nn.sigmoid(gates[:, D_MODEL:]) * y_fn
        x1 = x + dot(mixed.astype(BF16), wout_ref[...])
        yield
        h2 = _rms(x1, gmlp_ref[...]).astype(BF16)
        acc = x1
        up = dot(h2, wup_ref[:, 0:ff_chunk])
        yield
        for c in range(n_ff):
            a = jnp.maximum(up, 0.0)
            a = (a * a).astype(BF16)
            if c + 1 < n_ff:
                up = dot(h2, wup_ref[:, (c + 1) * ff_chunk:(c + 2) * ff_chunk])
            acc = acc + dot(a, wdn_ref[c * ff_chunk:(c + 1) * ff_chunk, :])
            yield
        y_ref[0, rows, :] = _rms(acc, gfin_ref[...])

    live = [chain(r) for r in range(n_sub)]
    stage = 0
    while live:
        for g in list(live):
            try:
                next(g)
            except StopIteration:
                live.remove(g)
        stage += 1
        if stage == 2:
            @pl.when(first)
            def _():
                for copy in mlp_weight_copies():
                    copy.wait()


def _trunk(x, o, f, gmix, wg, wna, wfn, wout, gmlp, wup, wdn, gfin, tm):
    B, T, D = x.shape
    assert T % tm == 0 and tm % ROW_BLOCK == 0 and D_FF % FF_CHUNK == 0
    body = functools.partial(_trunk_body, ff_chunk=FF_CHUNK, n_sub=tm // ROW_BLOCK)

    def const(arr):
        return pl.BlockSpec(arr.shape, lambda b, i: (0, 0), pipeline_mode=pl.Buffered(1))

    return pl.pallas_call(
        body,
        grid=(B, T // tm),
        in_specs=[
            pl.BlockSpec((1, tm, D), lambda b, i: (b, i, 0)),
            pl.BlockSpec((1, tm, NA_WIDTH), lambda b, i: (b, i, 0)),
            pl.BlockSpec((1, tm, FN_WIDTH), lambda b, i: (b, i, 0)),
            const(gmix), const(wg), const(wna), const(wfn), const(wout), const(gmlp),
            pl.BlockSpec(memory_space=pl.ANY), pl.BlockSpec(memory_space=pl.ANY), const(gfin),
        ],
        out_specs=pl.BlockSpec((1, tm, D), lambda b, i: (b, i, 0)),
        out_shape=jax.ShapeDtypeStruct((B, T, D), F32),
        scratch_shapes=[pltpu.VMEM(wup.shape, BF16), pltpu.VMEM(wdn.shape, BF16), pltpu.SemaphoreType.DMA((2,))],
        compiler_params=pltpu.CompilerParams(
            dimension_semantics=("arbitrary", "arbitrary"), vmem_limit_bytes=TRUNK_VMEM_LIMIT),
        name="trunk",
    )(x, o, f, gmix, wg, wna, wfn, wout, gmlp, wup, wdn, gfin)


def _run_group(x, qkvu, meta, consts):
    (cs, bias, g_mix, wg, wna, wfn, wout, g_mlp, wup, wdn, g_fin) = consts
    B, T, _ = x.shape
    n2 = (N_META + T) // DFT_OUTER
    q, k, v, u = qkvu
    _, km, vm, um = meta
    o = _attention(q, k, v, km, vm, bias, rows_per_step=min(T // GRID_W, ATTN_ROWS_PER_STEP))
    f = _fourier(u, um, *_fourier_constants(n2), cs)
    return _trunk(x, o, f, g_mix, wg, wna, wfn, wout, g_mlp, wup, wdn, g_fin, tm=TRUNK_TOKENS)


def kernel(x_prompt, x_sample, meta_tokens, w_in, rel_bias, meta_bias, w_branch_na, w_branch_fn,
           w_out, g_mix, g_mlp, w_up, w_down, g_final):
    assert w_in.shape[0] == 1, "single-layer trunk"
    w_in0 = w_in[0]
    split = 3 * NA_WIDTH + FN_WIDTH
    w_qkvu = w_in0[:, :split].astype(BF16)
    cs = _channel_dft_matrix()
    bias = _bias_tables(rel_bias[0], meta_bias[0])
    g_mix2 = g_mix[0][None]
    meta, _ = _inproj(meta_tokens[None], g_mix2, w_qkvu, tm=N_META)
    meta = tuple(m[0] for m in meta)
    later = [(w_in0, 2 * D_MODEL, split // (2 * D_MODEL)), (w_branch_na[0], D_MODEL, 0), (w_branch_fn[0], D_MODEL, 0),
             (w_out[0], D_MODEL, 0), (w_up[0], D_FF, 0), (w_down[0], D_MODEL, 0)]
    qkvu_prompt, (wg, wna, wfn, wout, wup, wdn) = _inproj(x_prompt, g_mix2, w_qkvu, tm=INPROJ_TOKENS, cast=later)
    qkvu_sample, _ = _inproj(x_sample, g_mix2, w_qkvu, tm=INPROJ_TOKENS)
    consts = (cs, bias, g_mix2, wg, wna, wfn, wout, g_mlp[0][None], wup, wdn, g_final[None])
    return (_run_group(x_prompt, qkvu_prompt, meta, consts), _run_group(x_sample, qkvu_sample, meta, consts))
```

```python
import functools
import math

import jax
import jax.numpy as jnp
import numpy as np
from jax import lax
from jax.experimental import pallas as pl
from jax.experimental.pallas import tpu as pltpu

F32 = jnp.float32
BF16 = jnp.bfloat16

D_MODEL = 1024
N_META = 16
GRID_W = 64
NA_HEADS = 8
NA_HEAD_DIM = 64
NA_WIDTH = NA_HEADS * NA_HEAD_DIM
NA_WIN_ROWS = 8
NA_WIN_COLS = 16
FN_GROUPS = 4
FN_GROUP_DIM = 128
FN_WIDTH = FN_GROUPS * FN_GROUP_DIM
D_FF = 4 * D_MODEL
RMS_EPS = 1e-6

LANES = 128
SUBLANES = 8
HEAD_PAIRS = NA_HEADS // 2
SCORES_AHEAD = 2
ROW_BLOCK = 256
NEG = -1e30
LOG2E = math.log2(math.e)
DFT_OUTER = 16
VMEM_LIMIT = 56 * 1024 * 1024
INPROJ_TOKENS = 8 * ROW_BLOCK
TRUNK_TOKENS = 4 * ROW_BLOCK
ATTN_ROWS_PER_STEP = 128
FF_CHUNK = 512
TRUNK_VMEM_LIMIT = 60 * 1024 * 1024


def _rms(x, g):
    ms = jnp.mean(x * x, axis=-1, keepdims=True)
    return x * lax.rsqrt(ms + RMS_EPS) * g


def _inproj_body(x_ref, g_ref, w_ref, *refs, n_sub, n_cast):
    cast_in, (q_ref, k_ref, v_ref, u_ref), cast_out = refs[:n_cast], refs[n_cast:n_cast + 4], refs[n_cast + 4:]
    sub = x_ref.shape[1] // n_sub
    for r in range(n_sub):
        rows = slice(r * sub, (r + 1) * sub)
        h = _rms(x_ref[0, rows, :], g_ref[...]).astype(BF16)
        z = jnp.dot(h, w_ref[...], preferred_element_type=F32)
        q_ref[0, rows, :] = (z[:, :NA_WIDTH] * (NA_HEAD_DIM ** -0.5 * LOG2E)).astype(BF16)
        k_ref[0, rows, :] = z[:, NA_WIDTH:2 * NA_WIDTH].astype(BF16)
        v_ref[0, rows, :] = z[:, 2 * NA_WIDTH:3 * NA_WIDTH].astype(BF16)
        u_ref[0, rows, :] = z[:, 3 * NA_WIDTH:]
    for src, dst in zip(cast_in, cast_out):
        dst[...] = src[...].astype(BF16)


def _inproj(x, g, w, tm, cast=()):
    B, T, D = x.shape
    assert T % tm == 0 and (tm % ROW_BLOCK == 0 or tm < ROW_BLOCK)
    steps = B * (T // tm)
    tok = pl.BlockSpec((1, tm, NA_WIDTH), lambda b, i: (b, i, 0))
    cast_specs, cast_shapes = [], []
    for mat, width, col in cast:
        rows = mat.shape[0] // steps
        assert mat.shape[0] % steps == 0 and rows % 16 == 0 and width % LANES == 0
        cast_specs.append(pl.BlockSpec((rows, width), lambda b, i, col=col: (b * (T // tm) + i, col)))
        cast_shapes.append(jax.ShapeDtypeStruct((mat.shape[0], width), BF16))
    out_cast_specs = [pl.BlockSpec(s.block_shape, lambda b, i: (b * (T // tm) + i, 0)) for s in cast_specs]
    outs = pl.pallas_call(
        functools.partial(_inproj_body, n_sub=max(tm // ROW_BLOCK, 1), n_cast=len(cast)),
        grid=(B, T // tm),
        in_specs=[
            pl.BlockSpec((1, tm, D), lambda b, i: (b, i, 0)),
            pl.BlockSpec((1, D), lambda b, i: (0, 0)),
            pl.BlockSpec(w.shape, lambda b, i: (0, 0)),
        ] + cast_specs,
        out_specs=[tok] * 4 + out_cast_specs,
        out_shape=[jax.ShapeDtypeStruct((B, T, NA_WIDTH), BF16)] * 3
        + [jax.ShapeDtypeStruct((B, T, FN_WIDTH), F32)] + cast_shapes,
        compiler_params=pltpu.CompilerParams(
            dimension_semantics=("arbitrary", "arbitrary"), vmem_limit_bytes=VMEM_LIMIT),
        name="inproj",
    )(x, g, w, *[mat for mat, _, _ in cast])
    return tuple(outs[:4]), tuple(outs[4:])


Q_ROWS = 8
Q_COLS = 16
K_ROWS = NA_WIN_ROWS + Q_ROWS - 1
K_COLS = 2 * NA_WIN_COLS
K_COL_START = (0, 8, 24, 32)
UNIT_KEYS = 512
UNIT_QUERIES = Q_ROWS * Q_COLS


def _attn_body(q_ref, k_ref, v_ref, km_ref, vm_ref, bias_ref, o_ref, *, rows, rows_per_step):
    chunk = pl.program_id(2)
    first_head = lax.broadcasted_iota(jnp.int32, (UNIT_QUERIES, LANES), 1) < NA_HEAD_DIM
    first_head_kv = lax.broadcasted_iota(jnp.int32, (UNIT_KEYS, LANES), 1) < NA_HEAD_DIM
    pad = jnp.zeros((UNIT_KEYS - K_ROWS * K_COLS - N_META, LANES), BF16)
    km = jnp.concatenate([km_ref[...], pad], axis=0)
    vm = jnp.concatenate([vm_ref[...], pad], axis=0)
    zero = jnp.zeros((UNIT_QUERIES, LANES), BF16)
    ones = jnp.ones((UNIT_KEYS, LANES), BF16)

    def window(ref, start_row, g):
        c0 = K_COL_START[g]
        pieces = []
        for w in range(K_ROWS):
            base = pl.multiple_of((start_row + w) * GRID_W, GRID_W)
            if c0 % 16 == 0:
                pieces.append(ref[0, pl.ds(base + c0, K_COLS), :])
            else:
                span = ref[0, pl.ds(base + c0 - SUBLANES, K_COLS + 16), :].astype(F32)
                pieces.append(span[SUBLANES:SUBLANES + K_COLS].astype(BF16))
        return pieces

    def scores(unit):
        blk, g = divmod(unit, GRID_W // Q_COLS)
        r0 = chunk * rows_per_step + Q_ROWS * blk
        start = jnp.clip(r0 - NA_WIN_ROWS // 2, 0, rows - K_ROWS)
        cfg = jnp.where(r0 == 0, 0, jnp.where(r0 == rows - Q_ROWS, 2, 1))
        q2 = jnp.concatenate(
            [q_ref[0, (Q_ROWS * blk + i) * GRID_W + Q_COLS * g:(Q_ROWS * blk + i) * GRID_W + Q_COLS * (g + 1), :]
             for i in range(Q_ROWS)], axis=0)
        qq = jnp.concatenate([jnp.where(first_head, q2, zero), jnp.where(first_head, zero, q2)], axis=0)
        kw = jnp.concatenate(window(k_ref, start, g) + [km], axis=0)
        s = lax.dot_general(qq, kw, (((1,), (1,)), ((), ())), preferred_element_type=F32)
        bias = jnp.concatenate([bias_ref[cfg, g, 0, 0], bias_ref[cfg, g, 0, 1]], axis=0)
        return s + bias, start

    def finish(unit, s, start):
        blk, g = divmod(unit, GRID_W // Q_COLS)
        vw = jnp.concatenate(window(v_ref, start, g) + [vm], axis=0)
        v0 = jnp.where(first_head_kv, vw, ones)
        v1 = jnp.where(first_head_kv, ones, vw)
        m = jnp.max(s, axis=-1, keepdims=True)
        p = jnp.exp2(s - m).astype(BF16)
        o0 = jnp.dot(p[:UNIT_QUERIES], v0, preferred_element_type=F32)
        o1 = jnp.dot(p[UNIT_QUERIES:], v1, preferred_element_type=F32)
        r0 = o0 / pltpu.roll(o0, NA_HEAD_DIM, axis=1)
        r1 = o1 / pltpu.roll(o1, NA_HEAD_DIM, axis=1)
        out = jnp.where(first_head, r0, r1).astype(BF16)
        for i in range(Q_ROWS):
            lo = (Q_ROWS * blk + i) * GRID_W + Q_COLS * g
            o_ref[0, lo:lo + Q_COLS, :] = out[i * Q_COLS:(i + 1) * Q_COLS]

    n_units = (rows_per_step // Q_ROWS) * (GRID_W // Q_COLS)
    queue = [scores(u) for u in range(min(SCORES_AHEAD, n_units))]
    for u in range(n_units):
        if u + SCORES_AHEAD < n_units:
            queue.append(scores(u + SCORES_AHEAD))
        finish(u, *queue.pop(0))


def _attention(q, k, v, km, vm, bias, rows_per_step):
    B, T, _ = q.shape
    rows = T // GRID_W
    assert T % GRID_W == 0 and rows % rows_per_step == 0 and rows_per_step % Q_ROWS == 0 and rows >= K_ROWS + Q_ROWS
    tq = rows_per_step * GRID_W
    body = functools.partial(_attn_body, rows=rows, rows_per_step=rows_per_step)
    return pl.pallas_call(
        body,
        grid=(B, HEAD_PAIRS, rows // rows_per_step),
        in_specs=[
            pl.BlockSpec((1, tq, LANES), lambda b, h, c: (b, c, h)),
            pl.BlockSpec((1, T, LANES), lambda b, h, c: (b, 0, h)),
            pl.BlockSpec((1, T, LANES), lambda b, h, c: (b, 0, h)),
            pl.BlockSpec((N_META, LANES), lambda b, h, c: (0, h)),
            pl.BlockSpec((N_META, LANES), lambda b, h, c: (0, h)),
            pl.BlockSpec(bias.shape[:2] + (1,) + bias.shape[3:], lambda b, h, c: (0, 0, h, 0, 0, 0)),
        ],
        out_specs=pl.BlockSpec((1, tq, LANES), lambda b, h, c: (b, c, h)),
        out_shape=jax.ShapeDtypeStruct((B, T, NA_WIDTH), BF16),
        compiler_params=pltpu.CompilerParams(
            dimension_semantics=("arbitrary", "arbitrary", "arbitrary"), vmem_limit_bytes=VMEM_LIMIT),
        name="attention",
    )(q, k, v, km, vm, bias)


UNIT_CONFIGS = ((0, tuple(max(i - 4, 0) for i in range(Q_ROWS))),
                (4, tuple(range(Q_ROWS))),
                (7, tuple(min(3 + i, 7) for i in range(Q_ROWS))))


def _bias_tables(rel_bias, meta_bias):
    n_groups = GRID_W // Q_COLS
    n_off = 2 * NA_WIN_COLS - 1
    n_row_off = 2 * NA_WIN_ROWS - 1
    hi = lax.Precision.HIGHEST
    onehot = np.zeros((n_groups, n_off, Q_COLS, K_COLS), np.float32)
    colmask = np.full((n_groups, Q_COLS, K_COLS), NEG, np.float32)
    for g in range(n_groups):
        for jq in range(Q_COLS):
            c = Q_COLS * g + jq
            cs = min(max(c - NA_WIN_COLS // 2, 0), GRID_W - NA_WIN_COLS)
            for x in range(K_COLS):
                j = K_COL_START[g] + x
                if cs <= j < cs + NA_WIN_COLS:
                    onehot[g, j - c + NA_WIN_COLS - 1, jq, x] = 1.0
                    colmask[g, jq, x] = 0.0
    blocks = jnp.einsum("hrd,gdjx->rghjx", rel_bias * LOG2E, onehot, precision=hi) + colmask[None, :, None]
    shape = (1, n_groups, NA_HEADS, Q_COLS, K_COLS)
    negblk = jnp.full(shape, NEG, F32)
    metablk = jnp.broadcast_to(jnp.concatenate(
        [meta_bias * LOG2E, jnp.full((NA_HEADS, K_COLS - N_META), NEG, F32)], axis=-1)[None, None, :, None, :], shape)
    blocks = jnp.concatenate([blocks, negblk, metablk], axis=0)
    slots = LANES // K_COLS
    placed = [jnp.pad(blocks, ((0, 0),) * 4 + ((K_COLS * kk, LANES - K_COLS * (kk + 1)),)) for kk in range(slots)]
    source = np.full((len(UNIT_CONFIGS), Q_ROWS, K_ROWS + 1), n_row_off, np.int32)
    for cc, (qslot, firsts) in enumerate(UNIT_CONFIGS):
        for i, w0 in enumerate(firsts):
            for w in range(w0, w0 + NA_WIN_ROWS):
                source[cc, i, w] = w - qslot - i + NA_WIN_ROWS - 1
            source[cc, i, K_ROWS] = n_row_off + 1

    def assemble(*refs):
        out_ref = refs[-1]
        for cc in range(source.shape[0]):
            for i in range(Q_ROWS):
                for vv in range((K_ROWS + 1) // slots):
                    acc = refs[0][int(source[cc, i, slots * vv])]
                    for kk in range(1, slots):
                        acc = acc + refs[kk][int(source[cc, i, slots * vv + kk])]
                    out_ref[cc, :, :, i * Q_COLS:(i + 1) * Q_COLS, vv * LANES:(vv + 1) * LANES] = acc

    t = pl.pallas_call(
        assemble,
        grid=(n_groups,),
        in_specs=[pl.BlockSpec((blocks.shape[0], 1) + blocks.shape[2:4] + (LANES,), lambda g: (0, g, 0, 0, 0))] * slots,
        out_specs=pl.BlockSpec((len(UNIT_CONFIGS), 1, NA_HEADS, UNIT_QUERIES, UNIT_KEYS), lambda g: (0, g, 0, 0, 0)),
        out_shape=jax.ShapeDtypeStruct((len(UNIT_CONFIGS), n_groups, NA_HEADS, UNIT_QUERIES, UNIT_KEYS), F32),
        compiler_params=pltpu.CompilerParams(dimension_semantics=("arbitrary",), vmem_limit_bytes=VMEM_LIMIT,
                                             allow_input_fusion=[True] * slots),
        name="bias_assemble",
    )(*placed)
    return t.reshape(t.shape[:2] + (HEAD_PAIRS, 2, UNIT_QUERIES, UNIT_KEYS))


def _snap(x):
    for v in (0.0, 1.0, -1.0):
        if abs(x - v) < 1e-12:
            return v
    return x


def _scaled(x, coef):
    if x is None or coef == 0.0:
        return None
    arr, sign = x
    coef = coef * sign
    if abs(coef) == 1.0:
        return arr, coef
    return arr * coef, 1.0


def _plus(a, b):
    if a is None:
        return b
    if b is None:
        return a
    (x, sx), (y, sy) = a, b
    if sx == sy:
        return x + y, sx
    return (x - y, 1.0) if sx > 0 else (y - x, 1.0)


def _negated(a):
    return None if a is None else (a[0], -a[1])


def _value(a):
    if a is None:
        return None
    return a[0] if a[1] > 0 else -a[0]


def _fft_real(xs):
    n = len(xs)
    if n == 1:
        return [(xs[0], None)]
    even, odd = _fft_real(xs[0::2]), _fft_real(xs[1::2])

    def bin_of(half, kk):
        if kk <= n // 4:
            return half[kk]
        re, im = half[n // 2 - kk]
        return re, _negated(im)

    out = []
    for kk in range(n // 2 + 1):
        c, s = _snap(math.cos(2 * math.pi * kk / n)), _snap(math.sin(2 * math.pi * kk / n))
        er, ei = bin_of(even, kk % (n // 2))
        orr, oi = bin_of(odd, kk % (n // 2))
        tr = _plus(_scaled(orr, c), _scaled(oi, s))
        ti = _plus(_scaled(oi, c), _scaled(orr, -s))
        out.append((_plus(er, tr), _plus(ei, ti)))
    return out


def _dft16_real(xs):
    return [(_value(re), _value(im)) for re, im in _fft_real([(x, 1.0) for x in xs])]


def _fourier_body(u_ref, um_ref, g_ref, gt_ref, twc_ref, tws_ref, cs_ref, f_ref,
                  zr_ref, zi_ref, sh_ref, *, n2):
    nm = n2 - 1
    half = DFT_OUTER // 2
    def stage1(r0, meta_rows):
        xs = [um_ref[pl.ds(r0, SUBLANES), :] if (a == 0 and meta_rows)
              else u_ref[0, pl.ds(n2 * a - N_META + r0, SUBLANES), :] for a in range(DFT_OUTER)]
        for c, (zr, zi) in enumerate(_dft16_real(xs)):
            zr_ref[c, pl.ds(r0, SUBLANES), :] = zr
            if zi is not None:
                zi_ref[c, pl.ds(r0, SUBLANES), :] = zi

    for i in range(N_META // SUBLANES):
        stage1(i * SUBLANES, True)

    def stage1_step(i, carry):
        stage1(pl.multiple_of(i * SUBLANES, SUBLANES), False)
        return carry

    lax.fori_loop(N_META // SUBLANES, nm // SUBLANES, stage1_step, 0, unroll=2)
    zt = _dft16_real([u_ref[0, pl.ds(n2 * a + nm - N_META, 1), :] for a in range(DFT_OUTER)])

    g = g_ref[...]
    gt = gt_ref[...]
    csm = cs_ref[...]
    inv_sqrt_n = 1.0 / math.sqrt(DFT_OUTER * n2)

    def prep(s):
        zr = zr_ref[s]
        ztr, zti = zt[s]
        zi = None if s in (0, half) else zi_ref[s]
        if s > 0:
            tc, ts = twc_ref[0:nm, s:s + 1], tws_ref[0:nm, s:s + 1]
            tct, tst = twc_ref[nm:n2, s:s + 1], tws_ref[nm:n2, s:s + 1]
            if zi is None:
                zr, zi = zr * tc, -(zr * ts)
                ztr, zti = ztr * tct, -(ztr * tst)
            else:
                zr, zi = zr * tc + zi * ts, zi * tc - zr * ts
                ztr, zti = ztr * tct + zti * tst, zti * tct - ztr * tst
        if zi is None:
            zi, zti = jnp.zeros_like(zr), jnp.zeros_like(ztr)
        z = jnp.concatenate([zr, zi], axis=1)
        tail = jnp.concatenate([ztr, zti], axis=1)
        return z, tail

    def dense(z, tail):
        p = jnp.dot(g, z.astype(BF16), preferred_element_type=F32)
        return p + gt * tail

    def channel_store(c, pr, pi):
        y = jnp.dot(jnp.concatenate([pr, pi], axis=1).astype(BF16), csm, preferred_element_type=F32)
        f_ref[0, pl.ds(c, nm, stride=DFT_OUTER), :] = y

    def post(s, p, z, tail):
        czr, czi, szr, szi = p[:nm, :LANES], p[:nm, LANES:], p[nm:, :LANES], p[nm:, LANES:]
        channel_store(s, czr + szi, czi - szr)
        if 0 < s < half:
            col = (jnp.sum(z, axis=0, keepdims=True) + tail) * inv_sqrt_n
            sh_ref[0, 0:nm] = czr - szi
            sh_ref[1, 0:nm] = -(czi + szr)
            sh_ref[0, nm:n2] = col[:, :LANES]
            sh_ref[1, nm:n2] = -col[:, LANES:]
            channel_store(DFT_OUTER - s, sh_ref[0, 1:n2], sh_ref[1, 1:n2])

    pending = None
    for s in range(half + 1):
        z, tail = prep(s)
        p = dense(z, tail)
        if pending is not None:
            post(*pending)
        pending = (s, p, z, tail)
    post(*pending)


def _fourier(u, um, g, gt, twc, tws, cs):
    B, T, _ = u.shape
    assert N_META == DFT_OUTER and T % (2 * SUBLANES * DFT_OUTER) == 0
    n2 = (N_META + T) // DFT_OUTER
    nm = n2 - 1
    body = functools.partial(_fourier_body, n2=n2)

    def const(arr):
        return pl.BlockSpec(arr.shape, lambda b, j: (0, 0))

    return pl.pallas_call(
        body,
        grid=(B, FN_GROUPS),
        in_specs=[
            pl.BlockSpec((1, T, LANES), lambda b, j: (b, 0, j)),
            pl.BlockSpec((N_META, LANES), lambda b, j: (0, j)),
            const(g), const(gt), const(twc), const(tws), const(cs),
        ],
        out_specs=pl.BlockSpec((1, T, LANES), lambda b, j: (b, 0, j)),
        out_shape=jax.ShapeDtypeStruct((B, T, FN_WIDTH), F32),
        scratch_shapes=[
            pltpu.VMEM((DFT_OUTER // 2 + 1, nm, LANES), F32),
            pltpu.VMEM((DFT_OUTER // 2 + 1, nm, LANES), F32),
            pltpu.VMEM((2, nm + SUBLANES, LANES), F32),
        ],
        compiler_params=pltpu.CompilerParams(
            dimension_semantics=("arbitrary", "arbitrary"), vmem_limit_bytes=VMEM_LIMIT),
        name="fourier",
    )(u, um, g, gt, twc, tws, cs)


def _fourier_constants(n2):
    n = DFT_OUTER * n2
    b = jnp.arange(n2, dtype=jnp.int32)
    d = jnp.arange(1, n2, dtype=jnp.int32)
    th = ((d[:, None] * b[None, :]) % n2).astype(F32) * (2.0 * math.pi / n2)
    full = jnp.concatenate([jnp.cos(th), jnp.sin(th)], axis=0) * (1.0 / math.sqrt(n))
    g, gt = full[:, :n2 - 1].astype(BF16), full[:, n2 - 1:]
    c = jnp.arange(DFT_OUTER, dtype=jnp.int32)
    tw = ((b[:, None] * c[None, :]) % n).astype(F32) * (2.0 * math.pi / n)
    return g, gt, jnp.cos(tw), jnp.sin(tw)


def _channel_dft_matrix():
    ch = jnp.arange(FN_GROUP_DIM, dtype=jnp.int32)
    th = ((ch[:, None] * ch[None, :]) % FN_GROUP_DIM).astype(F32) * (2.0 * math.pi / FN_GROUP_DIM)
    scale = 1.0 / math.sqrt(FN_GROUP_DIM)
    return jnp.concatenate([jnp.cos(th) * scale, jnp.sin(th) * scale], axis=0).astype(BF16)


def _trunk_body(x_ref, o_ref, f_ref, gmix_ref, wg_ref, wna_ref, wfn_ref, wout_ref, gmlp_ref,
                wup_ref, wdn_ref, gfin_ref, y_ref, *, ff_chunk, n_sub):
    def dot(a, b):
        return jnp.dot(a, b, preferred_element_type=F32)

    sub = x_ref.shape[1] // n_sub
    n_ff = D_FF // ff_chunk

    def chain(r):
        rows = slice(r * sub, (r + 1) * sub)
        x = x_ref[0, rows, :]
        h = _rms(x, gmix_ref[...]).astype(BF16)
        gates = dot(h, wg_ref[...])
        y_na = dot(o_ref[0, rows, :], wna_ref[...])
        y_fn = dot(f_ref[0, rows, :].astype(BF16), wfn_ref[...])
        yield
        mixed = jax.nn.sigmoid(gates[:, :D_MODEL]) * y_na + jax.nn.sigmoid(gates[:, D_MODEL:]) * y_fn
        x1 = x + dot(mixed.astype(BF16), wout_ref[...])
        yield
        h2 = _rms(x1, gmlp_ref[...]).astype(BF16)
        acc = x1
        up = dot(h2, wup_ref[:, 0:ff_chunk])
        yield
        for c in range(n_ff):
            a = jnp.maximum(up, 0.0)
            a = (a * a).astype(BF16)
            if c + 1 < n_ff:
                up = dot(h2, wup_ref[:, (c + 1) * ff_chunk:(c + 2) * ff_chunk])
            acc = acc + dot(a, wdn_ref[c * ff_chunk:(c + 1) * ff_chunk, :])
            yield
        y_ref[0, rows, :] = _rms(acc, gfin_ref[...])

    live = [chain(r) for r in range(n_sub)]
    while live:
        for g in list(live):
            try:
                next(g)
            except StopIteration:
                live.remove(g)


def _trunk(x, o, f, gmix, wg, wna, wfn, wout, gmlp, wup, wdn, gfin, tm):
    B, T, D = x.shape
    assert T % tm == 0 and tm % ROW_BLOCK == 0 and D_FF % FF_CHUNK == 0
    body = functools.partial(_trunk_body, ff_chunk=FF_CHUNK, n_sub=tm // ROW_BLOCK)

    def const(arr):
        return pl.BlockSpec(arr.shape, lambda b, i: (0, 0), pipeline_mode=pl.Buffered(1))

    return pl.pallas_call(
        body,
        grid=(B, T // tm),
        in_specs=[
            pl.BlockSpec((1, tm, D), lambda b, i: (b, i, 0)),
            pl.BlockSpec((1, tm, NA_WIDTH), lambda b, i: (b, i, 0)),
            pl.BlockSpec((1, tm, FN_WIDTH), lambda b, i: (b, i, 0)),
            const(gmix), const(wg), const(wna), const(wfn), const(wout), const(gmlp),
            const(wup), const(wdn), const(gfin),
        ],
        out_specs=pl.BlockSpec((1, tm, D), lambda b, i: (b, i, 0)),
        out_shape=jax.ShapeDtypeStruct((B, T, D), F32),
        compiler_params=pltpu.CompilerParams(
            dimension_semantics=("arbitrary", "arbitrary"), vmem_limit_bytes=TRUNK_VMEM_LIMIT),
        name="trunk",
    )(x, o, f, gmix, wg, wna, wfn, wout, gmlp, wup, wdn, gfin)


def _run_group(x, qkvu, meta, consts):
    (cs, bias, g_mix, wg, wna, wfn, wout, g_mlp, wup, wdn, g_fin) = consts
    B, T, _ = x.shape
    n2 = (N_META + T) // DFT_OUTER
    q, k, v, u = qkvu
    _, km, vm, um = meta
    o = _attention(q, k, v, km, vm, bias, rows_per_step=min(T // GRID_W, ATTN_ROWS_PER_STEP))
    f = _fourier(u, um, *_fourier_constants(n2), cs)
    return _trunk(x, o, f, g_mix, wg, wna, wfn, wout, g_mlp, wup, wdn, g_fin, tm=TRUNK_TOKENS)


def kernel(x_prompt, x_sample, meta_tokens, w_in, rel_bias, meta_bias, w_branch_na, w_branch_fn,
           w_out, g_mix, g_mlp, w_up, w_down, g_final):
    assert w_in.shape[0] == 1, "single-layer trunk"
    w_in0 = w_in[0]
    split = 3 * NA_WIDTH + FN_WIDTH
    w_qkvu = w_in0[:, :split].astype(BF16)
    cs = _channel_dft_matrix()
    bias = _bias_tables(rel_bias[0], meta_bias[0])
    g_mix2 = g_mix[0][None]
    meta, _ = _inproj(meta_tokens[None], g_mix2, w_qkvu, tm=N_META)
    meta = tuple(m[0] for m in meta)
    later = [(w_in0, 2 * D_MODEL, split // (2 * D_MODEL)), (w_branch_na[0], D_MODEL, 0), (w_branch_fn[0], D_MODEL, 0),
             (w_out[0], D_MODEL, 0), (w_up[0], D_FF, 0), (w_down[0], D_MODEL, 0)]
    qkvu_prompt, (wg, wna, wfn, wout, wup, wdn) = _inproj(x_prompt, g_mix2, w_qkvu, tm=INPROJ_TOKENS, cast=later)
    qkvu_sample, _ = _inproj(x_sample, g_mix2, w_qkvu, tm=INPROJ_TOKENS)
    consts = (cs, bias, g_mix2, wg, wna, wfn, wout, g_mlp[0][None], wup, wdn, g_final[None])
    return (_run_group(x_prompt, qkvu_prompt, meta, consts), _run_group(x_sample, qkvu_sample, meta, consts))
```
